```python
import math
import jax
import jax.numpy as jnp
from jax import lax
import numpy as np

D_MODEL = 1024
BATCH = 4
SEQ = 4096
DEPTH = 4

N_EVEN = (DEPTH + 1) // 2
N_ODD = DEPTH // 2
RMS_EPS = 1e-6

A_WIDTH = D_MODEL
A_HEAD_DIM = 64
A_HEADS = A_WIDTH // A_HEAD_DIM
A_PATTERNS = ((128, 1), (512, 4), (2048, 16))
A_BLOCK = 128

B_WIDTH = D_MODEL
B_EXPAND = 128
B_HEADS = B_WIDTH // B_EXPAND
B_HEAD_DIM = B_WIDTH // B_HEADS
B_CHUNK = 64

EVEN_SIZES = (A_WIDTH, A_WIDTH, A_WIDTH, A_WIDTH, B_WIDTH, B_WIDTH, B_WIDTH, B_WIDTH)
EVEN_IN = sum(EVEN_SIZES)
EVEN_MIX = A_WIDTH + B_WIDTH

C_HEAD_K = 128
C_HEAD_V = 128
C_K_HEADS = D_MODEL // C_HEAD_K
C_V_HEADS = 2 * C_K_HEADS
C_KEY_WIDTH = C_K_HEADS * C_HEAD_K
C_VAL_WIDTH = C_V_HEADS * C_HEAD_V
C_CONV = 4
C_CHUNK = 64
C_CONV_CH = 2 * C_KEY_WIDTH + C_VAL_WIDTH
ODD_SIZES = (C_CONV_CH, C_VAL_WIDTH, C_V_HEADS, C_V_HEADS)
ODD_IN = sum(ODD_SIZES)

kernel_name = 'hybrid_dilated_hgrn2_gdn_trunk'


def _split(t, sizes):
    return jnp.split(t, np.cumsum(sizes)[:-1].tolist(), axis=-1)


def rms_norm(x, w):
    xf = x.astype(jnp.float32)
    y = xf * lax.rsqrt(jnp.mean(xf * xf, axis=-1, keepdims=True) + RMS_EPS)
    return (y * w.astype(jnp.float32)).astype(x.dtype)


def l2_normalize(x):
    return x * lax.rsqrt(jnp.sum(x * x, axis=-1, keepdims=True) + RMS_EPS)


def dilated_branch(q, k, v, window, dilation):
    bsz, s, h, dh = q.shape
    L = s // dilation
    nb = -(-L // A_BLOCK)
    Lp = nb * A_BLOCK
    steps = window // dilation

    def to_residue(t):
        t = t.reshape(bsz, L, dilation, h, dh).transpose(0, 2, 1, 3, 4)
        return jnp.pad(t, ((0, 0), (0, 0), (0, Lp - L), (0, 0), (0, 0)))

    def key_windows(t):
        t = jnp.pad(to_residue(t), ((0, 0), (0, 0), (A_BLOCK, 0), (0, 0), (0, 0)))
        t = t.reshape(bsz, dilation, nb + 1, A_BLOCK, h, dh)
        return jnp.concatenate([t[:, :, :-1], t[:, :, 1:]], axis=3)

    qr = to_residue(q).reshape(bsz, dilation, nb, A_BLOCK, h, dh)
    kw = key_windows(k)
    vw = key_windows(v)
    qpos = jnp.arange(nb)[:, None] * A_BLOCK + jnp.arange(A_BLOCK)[None, :]
    kpos = jnp.arange(nb)[:, None] * A_BLOCK - A_BLOCK + jnp.arange(2 * A_BLOCK)[None, :]
    dist = qpos[:, :, None] - kpos[:, None, :]
    valid = (dist >= 0) & (dist <= steps) & (kpos[:, None, :] >= 0)

    sc = jnp.einsum('brnqhd,brnkhd->brnhqk', qr, kw).astype(jnp.float32) * (A_HEAD_DIM ** -0.5)
    sc = jnp.where(valid[None, None, :, None], sc, -jnp.inf)
    m = jnp.max(sc, axis=-1, keepdims=True)
    p = jnp.exp(sc - m)
    den = jnp.sum(p, axis=-1, keepdims=True)
    o = jnp.einsum('brnhqk,brnkhd->brnqhd', p / den, vw.astype(jnp.float32))
    lse = jnp.swapaxes((m + jnp.log(den))[..., 0], -1, -2)

    def from_residue(t):
        t = t.reshape((bsz, dilation, Lp) + t.shape[4:])[:, :, :L]
        t = jnp.swapaxes(t, 1, 2)
        return t.reshape((bsz, s) + t.shape[3:])

    return from_residue(o), from_residue(lse)


def dilated_mixture_attention(q, k, v):
    outs, lses = [], []
    for window, dilation in A_PATTERNS:
        o, lse = dilated_branch(q, k, v, window, dilation)
        outs.append(o)
        lses.append(lse)
    wts = jax.nn.softmax(jnp.stack(lses), axis=0)
    return jnp.einsum('gbsh,gbshd->bshd', wts, jnp.stack(outs))


def hgrn2_chunked(q, f_logit, i, lb, norm_w):
    bsz, s, h, e = q.shape
    n = s // B_CHUNK
    q = q.astype(jnp.float32)
    i = i.astype(jnp.float32)
    lbh = lb.reshape(h, e)
    log_f = jnp.logaddexp(jnp.log(lbh), jnp.log1p(-lbh) + jax.nn.log_sigmoid(f_logit.astype(jnp.float32)))
    k = -jnp.expm1(log_f)

    def chunks(t):
        return t.reshape(bsz, n, B_CHUNK, h, t.shape[-1]).transpose(1, 0, 3, 2, 4)

    qc, kc, vc = chunks(q), chunks(k), chunks(i)
    gc = jnp.cumsum(chunks(log_f), axis=-2)
    causal = jnp.tril(jnp.ones((B_CHUNK, B_CHUNK), dtype=bool))

    def step(state, inp):
        qb, kb, vb, gb = inp
        o_inter = jnp.einsum('bhce,bhev->bhcv', qb * jnp.exp(gb), state)
        diff = gb[:, :, :, None, :] - gb[:, :, None, :, :]
        decay = jnp.exp(jnp.where(causal[:, :, None], diff, -jnp.inf))
        attn = jnp.einsum('bhie,bhje,bhije->bhij', qb, kb, decay)
        o = o_inter + jnp.einsum('bhij,bhjv->bhiv', attn, vb)
        g_last = gb[:, :, -1]
        k_tail = kb * jnp.exp(g_last[:, :, None, :] - gb)
        new_state = state * jnp.exp(g_last)[..., None] + jnp.einsum('bhce,bhcv->bhev', k_tail, vb)
        return new_state, o

    state0 = jnp.zeros((bsz, h, e, i.shape[-1]), jnp.float32)
    _, o = lax.scan(step, state0, (qc, kc, vc, gc))
    o = o.transpose(1, 0, 3, 2, 4).reshape(bsz, s, h, i.shape[-1])
    return rms_norm(o, norm_w)


def even_mixer(hn, w_in, w_out, lb, hgrn_norm_w):
    bsz, s, _ = hn.shape
    aq, ak, av, ag, bq, bf, bi, bg = _split(hn @ w_in, EVEN_SIZES)
    shp_a = (bsz, s, A_HEADS, A_HEAD_DIM)
    a_out = dilated_mixture_attention(aq.reshape(shp_a), ak.reshape(shp_a), av.reshape(shp_a))
    b_out = hgrn2_chunked(bq.reshape(bsz, s, B_HEADS, B_EXPAND), bf.reshape(bsz, s, B_HEADS, B_EXPAND),
                          bi.reshape(bsz, s, B_HEADS, B_HEAD_DIM), lb, hgrn_norm_w)
    mixed = jnp.concatenate([
        a_out.reshape(bsz, s, A_WIDTH).astype(hn.dtype) * jax.nn.silu(ag),
        b_out.reshape(bsz, s, B_WIDTH).astype(hn.dtype) * jax.nn.silu(bg)], axis=-1)
    return mixed @ w_out


def causal_depthwise_conv(x, w):
    return lax.conv_general_dilated(x, w[:, None, :].astype(x.dtype), window_strides=(1,),
                                    padding=[(w.shape[0] - 1, 0)],
                                    dimension_numbers=('NWC', 'WIO', 'NWC'),
                                    feature_group_count=x.shape[-1])


def gated_delta_rule_chunked(q, k, v, g, beta):
    bsz, s, h, dk = q.shape
    dv = v.shape[-1]
    n = s // C_CHUNK

    def chunks(t):
        return t.reshape(bsz, n, C_CHUNK, h, -1).transpose(0, 3, 1, 2, 4)

    qc, kc, vc = chunks(q), chunks(k), chunks(v)
    bc = chunks(beta[..., None])
    gcum = jnp.cumsum(chunks(g[..., None])[..., 0], axis=-1)
    causal = jnp.tril(jnp.ones((C_CHUNK, C_CHUNK), dtype=bool))
    strict = jnp.tril(jnp.ones((C_CHUNK, C_CHUNK), dtype=bool), -1)
    decay = jnp.exp(jnp.where(causal, gcum[..., :, None] - gcum[..., None, :], -jnp.inf))
    kb = kc * bc
    lower = jnp.where(strict, jnp.einsum('bhnid,bhnjd->bhnij', kb, kc) * decay, 0.0)
    a_mat = lower + jnp.eye(C_CHUNK, dtype=jnp.float32)
    rhs = jnp.concatenate([vc * bc, kb * jnp.exp(gcum)[..., None]], axis=-1)
    sol = lax.linalg.triangular_solve(a_mat, rhs, left_side=True, lower=True, unit_diagonal=True)
    u, w = sol[..., :dv], sol[..., dv:]
    attn_qk = jnp.einsum('bhnid,bhnjd->bhnij', qc, kc) * decay
    q_dec = qc * jnp.exp(gcum)[..., None]
    k_tail = kc * jnp.exp(gcum[..., -1:] - gcum)[..., None]
    g_last = jnp.exp(gcum[..., -1])

    def step(state, inp):
        u_n, w_n, a_n, qd_n, kt_n, gl_n = inp
        v_new = u_n - jnp.einsum('bhcd,bhdv->bhcv', w_n, state)
        o = jnp.einsum('bhcd,bhdv->bhcv', qd_n, state) + jnp.einsum('bhij,bhjv->bhiv', a_n, v_new)
        new_state = state * gl_n[..., None, None] + jnp.einsum('bhcd,bhcv->bhdv', kt_n, v_new)
        return new_state, o

    xs = tuple(jnp.moveaxis(t, 2, 0) for t in (u, w, attn_qk, q_dec, k_tail, g_last))
    state0 = jnp.zeros((bsz, h, dk, dv), jnp.float32)
    _, o = lax.scan(step, state0, xs)
    return o.transpose(1, 0, 3, 2, 4).reshape(bsz, s, h, dv)


def odd_mixer(hn, w_in, conv_w, dt_bias, a_log, norm_w, w_out):
    bsz, s, _ = hn.shape
    qkv, z, b_logit, a_logit = _split(hn @ w_in, ODD_SIZES)
    qkv = jax.nn.silu(causal_depthwise_conv(qkv, conv_w)).astype(jnp.float32)
    q, k, v = _split(qkv, (C_KEY_WIDTH, C_KEY_WIDTH, C_VAL_WIDTH))
    rep = C_V_HEADS // C_K_HEADS
    q = jnp.repeat(l2_normalize(q.reshape(bsz, s, C_K_HEADS, C_HEAD_K)), rep, axis=2) * (C_HEAD_K ** -0.5)
    k = jnp.repeat(l2_normalize(k.reshape(bsz, s, C_K_HEADS, C_HEAD_K)), rep, axis=2)
    v = v.reshape(bsz, s, C_V_HEADS, C_HEAD_V)
    beta = jax.nn.sigmoid(b_logit.astype(jnp.float32))
    g = -jnp.exp(a_log.astype(jnp.float32)) * jax.nn.softplus(a_logit.astype(jnp.float32) + dt_bias.astype(jnp.float32))
    o = gated_delta_rule_chunked(q, k, v, g, beta)
    o = rms_norm(o, norm_w) * jax.nn.silu(z.astype(jnp.float32).reshape(bsz, s, C_V_HEADS, C_HEAD_V))
    return o.reshape(bsz, s, C_VAL_WIDTH).astype(hn.dtype) @ w_out


def setup_inputs(seed: int = 0) -> dict:
    key = jax.random.key(seed)
    ks = jax.random.split(key, 16)
    f32 = jnp.float32
    nrm = lambda k, shp, sc: jax.random.normal(k, shp, f32) * sc
    dt = jnp.exp(jax.random.uniform(ks[10], (N_ODD, C_V_HEADS), f32, math.log(1e-3), math.log(1e-1)))
    return {
        'x': nrm(ks[0], (BATCH, SEQ, D_MODEL), 1.0),
        'norm_w': 1.0 + nrm(ks[1], (DEPTH, D_MODEL), 0.02),
        'final_norm_w': 1.0 + nrm(ks[2], (D_MODEL,), 0.02),
        'even_w_in': nrm(ks[3], (N_EVEN, D_MODEL, EVEN_IN), D_MODEL ** -0.5),
        'even_w_out': nrm(ks[4], (N_EVEN, EVEN_MIX, D_MODEL), EVEN_MIX ** -0.5),
        'hgrn_lb_logits': nrm(ks[5], (N_EVEN, B_HEADS * B_EXPAND), 0.1),
        'hgrn_norm_w': 1.0 + nrm(ks[6], (N_EVEN, B_HEAD_DIM), 0.02),
        'odd_w_in': nrm(ks[7], (N_ODD, D_MODEL, ODD_IN), D_MODEL ** -0.5),
        'odd_conv_w': nrm(ks[8], (N_ODD, C_CONV, C_CONV_CH), C_CONV ** -0.5),
        'odd_dt_bias': jnp.log(jnp.expm1(dt)),
        'odd_a_log': jnp.log(jax.random.uniform(ks[9], (N_ODD, C_V_HEADS), f32, 1.0, 16.0)),
        'odd_norm_w': 1.0 + nrm(ks[11], (N_ODD, C_HEAD_V), 0.02),
        'odd_w_out': nrm(ks[12], (N_ODD, C_VAL_WIDTH, D_MODEL), C_VAL_WIDTH ** -0.5),
    }


def reference(x, norm_w, final_norm_w, even_w_in, even_w_out, hgrn_lb_logits, hgrn_norm_w,
              odd_w_in, odd_conv_w, odd_dt_bias, odd_a_log, odd_norm_w, odd_w_out):
    lb_all = jnp.cumsum(jax.nn.softmax(hgrn_lb_logits.astype(jnp.float32), axis=0), axis=0)
    lb_all = jnp.maximum(lb_all - lb_all[0:1], 0.0)
    h = x
    for layer in range(DEPTH):
        hn = rms_norm(h, norm_w[layer])
        j = layer // 2
        if layer % 2 == 0:
            h = h + even_mixer(hn, even_w_in[j], even_w_out[j], lb_all[j], hgrn_norm_w[j])
        else:
            h = h + odd_mixer(hn, odd_w_in[j], odd_conv_w[j], odd_dt_bias[j], odd_a_log[j],
                              odd_norm_w[j], odd_w_out[j])
    return rms_norm(h, final_norm_w)
```

```python
import functools

import jax
import jax.numpy as jnp
from jax import lax
from jax.experimental import pallas as pl
from jax.experimental.pallas import tpu as pltpu

F32 = jnp.float32
BF16 = jnp.bfloat16
HIGHEST = lax.Precision.HIGHEST

RMS_EPS = 1e-6
LANES = 128
SUBLANES = 8
VMEM_LIMIT = 48 * 1024 * 1024

D_MODEL = 1024
A_HEAD_DIM = 64
A_BLOCK = 128
A_DILATIONS = (1, 4, 16)
CHUNK = 64
SUB = 16
HEAD = 128
CONV_K = 4

EVEN_IN = 8 * D_MODEL
ODD_MAIN = 6 * D_MODEL
ODD_IN_PAD = ODD_MAIN + LANES
N_V_HEADS = 16


def _dot(a, b):
    return jnp.dot(a, b, preferred_element_type=F32)


def _dot_nt(a, b, precision=None):
    return lax.dot_general(a, b, (((1,), (1,)), ((), ())), precision=precision,
                           preferred_element_type=F32)


def _dot_tn(a, b):
    return lax.dot_general(a, b, (((0,), (0,)), ((), ())), preferred_element_type=F32)


def _bf(x):
    return x.astype(BF16)


def _silu(x):
    return x * jax.nn.sigmoid(x)


def _softplus(x):
    return jnp.maximum(x, 0.0) + jnp.log1p(jnp.exp(-jnp.abs(x)))


def _params(*sem):
    return pltpu.CompilerParams(dimension_semantics=sem, vmem_limit_bytes=VMEM_LIMIT)


def _norm_matmul_kernel(x_ref, nw_ref, w_ref, o_ref, xn_ref):
    @pl.when(pl.program_id(1) == 0)
    def _():
        x = x_ref[...]
        ms = jnp.mean(x * x, axis=-1, keepdims=True)
        xn_ref[...] = _bf(x * lax.rsqrt(ms + RMS_EPS) * nw_ref[...])

    o_ref[...] = _dot(xn_ref[...], w_ref[...])


def _norm_matmul(x, nw, w, tm, tn):
    t, d = x.shape
    n = w.shape[1]
    return pl.pallas_call(
        _norm_matmul_kernel,
        grid=(t // tm, n // tn),
        in_specs=[pl.BlockSpec((tm, d), lambda i, j: (i, 0)),
                  pl.BlockSpec((1, d), lambda i, j: (0, 0)),
                  pl.BlockSpec((d, tn), lambda i, j: (0, j))],
        out_specs=pl.BlockSpec((tm, tn), lambda i, j: (i, j)),
        out_shape=jax.ShapeDtypeStruct((t, n), F32),
        scratch_shapes=[pltpu.VMEM((tm, d), BF16)],
        compiler_params=_params("parallel", "arbitrary"),
        name="norm_in_proj",
    )(x, nw.reshape(1, d), w)


def _out_proj_kernel(*refs, n_in, final):
    h_ref = refs[0]
    a_refs = refs[1:1 + n_in]
    w_refs = refs[1 + n_in:1 + 2 * n_in]
    o_ref = refs[-1]
    acc = h_ref[...]
    for a_ref, w_ref in zip(a_refs, w_refs):
        acc = acc + _dot(a_ref[...], w_ref[...])
    if final:
        fw_ref = refs[1 + 2 * n_in]
        ms = jnp.mean(acc * acc, axis=-1, keepdims=True)
        acc = acc * lax.rsqrt(ms + RMS_EPS) * fw_ref[...]
    o_ref[...] = acc


def _out_proj(h, acts, ws, final_w, tm):
    t, d = h.shape
    n_in = len(acts)
    final = final_w is not None
    in_specs = [pl.BlockSpec((tm, d), lambda i: (i, 0))]
    in_specs += [pl.BlockSpec((tm, a.shape[1]), lambda i: (i, 0)) for a in acts]
    in_specs += [pl.BlockSpec(w.shape, lambda i: (0, 0)) for w in ws]
    args = [h, *acts, *ws]
    if final:
        in_specs.append(pl.BlockSpec((1, d), lambda i: (0, 0)))
        args.append(final_w.reshape(1, d))
    return pl.pallas_call(
        functools.partial(_out_proj_kernel, n_in=n_in, final=final),
        grid=(t // tm,),
        in_specs=in_specs,
        out_specs=pl.BlockSpec((tm, d), lambda i: (i, 0)),
        out_shape=jax.ShapeDtypeStruct((t, d), F32),
        compiler_params=_params("parallel"),
        name="out_proj",
    )(*args)


def _attn_branch_kernel(*refs, tq, first, last):
    q_ref, k_ref, kp_ref, v_ref, vp_ref = refs[:5]
    pos = 5
    if not first:
        acc_in, m_in, l_in = refs[pos:pos + 3]
        pos += 3
    if last:
        gate_ref = refs[pos]
        pos += 1
        (o_ref,) = refs[pos:]
    else:
        acc_out, m_out, l_out = refs[pos:]

    qt = pl.program_id(3)
    row = lax.broadcasted_iota(jnp.int32, (A_BLOCK, 2 * A_BLOCK), 0)
    col = lax.broadcasted_iota(jnp.int32, (A_BLOCK, 2 * A_BLOCK), 1)
    dist = row - col + A_BLOCK
    band = (dist >= 0) & (dist <= A_BLOCK)
    band_first = band & ((col >= A_BLOCK) | (qt > 0))
    lane = lax.broadcasted_iota(jnp.int32, (A_BLOCK, LANES), 1)
    head0 = lane < A_HEAD_DIM

    for i in range(tq // A_BLOCK):
        rows = slice(i * A_BLOCK, (i + 1) * A_BLOCK)
        q = q_ref[rows, :]
        if i == 0:
            kw = jnp.concatenate([kp_ref[...], k_ref[rows, :]], axis=0)
            vw = jnp.concatenate([vp_ref[...], v_ref[rows, :]], axis=0)
            mask = band_first
        else:
            kw = k_ref[(i - 1) * A_BLOCK:(i + 1) * A_BLOCK, :]
            vw = v_ref[(i - 1) * A_BLOCK:(i + 1) * A_BLOCK, :]
            mask = band
        kw = _bf(kw)
        vw = _bf(vw)
        if not first:
            m_prev = m_in[rows, :]
        parts = []
        for h in range(2):
            qh = _bf(jnp.where(head0 if h == 0 else ~head0, q, 0.0))
            s = _dot_nt(qh, kw) * (A_HEAD_DIM ** -0.5)
            s = jnp.where(mask, s, -jnp.inf)
            m_new = jnp.max(s, axis=-1, keepdims=True)
            if not first:
                m_new = jnp.maximum(m_new, m_prev[:, h * A_HEAD_DIM:h * A_HEAD_DIM + 1])
            p = jnp.exp(s - m_new)
            parts.append((m_new, jnp.sum(p, axis=-1, keepdims=True), _dot(_bf(p), vw)))
        m_new = jnp.where(head0, parts[0][0], parts[1][0])
        l_new = jnp.where(head0, parts[0][1], parts[1][1])
        acc = jnp.where(head0, parts[0][2], parts[1][2])
        if not first:
            alpha = jnp.exp(m_prev - m_new)
            acc = acc_in[rows, :] * alpha + acc
            l_new = l_in[rows, :] * alpha + l_new
        if last:
            o_ref[rows, :] = _bf(acc / l_new * _silu(gate_ref[rows, :]))
        else:
            acc_out[rows, :] = acc
            m_out[rows, :] = m_new
            l_out[rows, :] = l_new


def _attn_branch(proj, state, dilation, first, last):
    bsz, s, n_proj = proj.shape
    length = s // dilation
    tq = min(length, 1024)
    hp_blocks = D_MODEL // LANES
    pb = n_proj // LANES
    view = proj.reshape(bsz, length, dilation * n_proj)
    qb = tq // A_BLOCK

    def col_spec(off):
        return pl.BlockSpec((None, tq, LANES),
                            lambda b, hp, r, t: (b, t, r * pb + off * hp_blocks + hp))

    def prev_spec(off):
        return pl.BlockSpec((None, A_BLOCK, LANES),
                            lambda b, hp, r, t: (b, jnp.maximum(t * qb - 1, 0),
                                                 r * pb + off * hp_blocks + hp))

    state_spec = pl.BlockSpec((None, tq, LANES), lambda b, hp, r, t: (b, t, r * hp_blocks + hp))
    in_specs = [col_spec(0), col_spec(1), prev_spec(1), col_spec(2), prev_spec(2)]
    args = [view, view, view, view, view]
    if not first:
        in_specs += [state_spec] * 3
        args += [x.reshape(bsz, length, dilation * D_MODEL) for x in state]
    if last:
        in_specs.append(col_spec(3))
        args.append(view)
        out_specs = state_spec
        out_shape = jax.ShapeDtypeStruct((bsz, length, dilation * D_MODEL), BF16)
    else:
        out_specs = [state_spec] * 3
        out_shape = [jax.ShapeDtypeStruct((bsz, length, dilation * D_MODEL), F32)] * 3
    out = pl.pallas_call(
        functools.partial(_attn_branch_kernel, tq=tq, first=first, last=last),
        grid=(bsz, hp_blocks, dilation, length // tq),
        in_specs=in_specs,
        out_specs=out_specs,
        out_shape=out_shape,
        compiler_params=_params("parallel", "parallel", "parallel", "parallel"),
        name=f"dilated_attn_d{dilation}",
    )(*args)
    if last:
        return out.reshape(bsz, s, D_MODEL)
    return [x.reshape(bsz, s, D_MODEL) for x in out]


def _dilated_attention(proj):
    state = None
    n = len(A_DILATIONS)
    for g, dilation in enumerate(A_DILATIONS):
        state = _attn_branch(proj, state, dilation, first=(g == 0), last=(g == n - 1))
    return state


def _hgrn_kernel(q_ref, f_ref, i_ref, gate_ref, loglb_ref, log1mlb_ref, nw_ref, o_ref, st_ref, *, tb):
    @pl.when(pl.program_id(2) == 0)
    def _():
        st_ref[...] = jnp.zeros_like(st_ref)

    r64 = lax.broadcasted_iota(jnp.int32, (CHUNK, CHUNK), 0)
    c64 = lax.broadcasted_iota(jnp.int32, (CHUNK, CHUNK), 1)
    tri = jnp.where(r64 >= c64, 1.0, 0.0).astype(F32)
    rsub = lax.broadcasted_iota(jnp.int32, (SUB, HEAD), 0)
    lsub = lax.broadcasted_iota(jnp.int32, (SUB, SUB), 1)
    log_lb = loglb_ref[...]
    log_1mlb = log1mlb_ref[...]
    state = st_ref[...]

    for c in range(tb // CHUNK):
        rows = slice(c * CHUNK, (c + 1) * CHUNK)
        q = q_ref[rows, :]
        x = f_ref[rows, :]
        v = i_ref[rows, :]
        log_sig = jnp.minimum(x, 0.0) - jnp.log1p(jnp.exp(-jnp.abs(x)))
        a = log_lb
        b = log_1mlb + log_sig
        log_f = jnp.maximum(a, b) + jnp.log1p(jnp.exp(-jnp.abs(a - b)))
        k = jnp.exp(log_1mlb + log_sig - x)
        g = jnp.dot(tri, log_f, precision=HIGHEST, preferred_element_type=F32)
        g_last = g[CHUNK - 1:CHUNK, :]
        vb = _bf(v)

        o_inter = _dot_nt(_bf(q * jnp.exp(g)), _bf(state))

        o_rows = []
        for sb in range(CHUNK // SUB):
            lo = sb * SUB
            qs = q[lo:lo + SUB, :]
            gs = g[lo:lo + SUB, :]
            diag = jnp.zeros((SUB, SUB), F32)
            for j in range(SUB):
                kj = k[lo + j:lo + j + 1, :]
                gj = g[lo + j:lo + j + 1, :]
                t = jnp.where(rsub >= j, qs * kj * jnp.exp(jnp.minimum(gs - gj, 0.0)), 0.0)
                diag = jnp.where(lsub == j, jnp.sum(t, axis=-1, keepdims=True), diag)
            o_sb = _dot(_bf(diag), vb[lo:lo + SUB, :])
            if sb > 0:
                g_ref = g[lo - 1:lo, :]
                qd = qs * jnp.exp(gs - g_ref)
                kd = k[:lo, :] * jnp.exp(g_ref - g[:lo, :])
                o_sb = o_sb + _dot(_bf(_dot_nt(_bf(qd), _bf(kd))), vb[:lo, :])
            o_rows.append(o_sb)
        o = o_inter + jnp.concatenate(o_rows, axis=0)

        k_tail = k * jnp.exp(g_last - g)
        state = state * jnp.exp(g_last) + _dot_tn(vb, _bf(k_tail))

        ms = jnp.mean(o * o, axis=-1, keepdims=True)
        o = o * lax.rsqrt(ms + RMS_EPS) * nw_ref[...]
        o_ref[rows, :] = _bf(o * _silu(gate_ref[rows, :]))

    st_ref[...] = state


def _hgrn(proj, log_lb, log_1mlb, norm_w, tb):
    bsz, s, _ = proj.shape
    heads = D_MODEL // HEAD
    base = 4 * heads

    def col_spec(off):
        return pl.BlockSpec((None, tb, HEAD), lambda b, h, t: (b, t, base + off * heads + h))

    vec_spec = pl.BlockSpec((None, 1, HEAD), lambda b, h, t: (h, 0, 0))
    return pl.pallas_call(
        functools.partial(_hgrn_kernel, tb=tb),
        grid=(bsz, heads, s // tb),
        in_specs=[col_spec(0), col_spec(1), col_spec(2), col_spec(3), vec_spec, vec_spec,
                  pl.BlockSpec((1, HEAD), lambda b, h, t: (0, 0))],
        out_specs=pl.BlockSpec((None, tb, HEAD), lambda b, h, t: (b, t, h)),
        out_shape=jax.ShapeDtypeStruct((bsz, s, D_MODEL), BF16),
        scratch_shapes=[pltpu.VMEM((HEAD, HEAD), F32)],
        compiler_params=_params("parallel", "parallel", "arbitrary"),
        name="hgrn2",
    )(proj, proj, proj, proj, log_lb.reshape(heads, 1, HEAD), log_1mlb.reshape(heads, 1, HEAD),
      norm_w.reshape(1, HEAD))


def _unit_lower_inverse(low):
    r = lax.broadcasted_iota(jnp.int32, (CHUNK, CHUNK), 0)
    c = lax.broadcasted_iota(jnp.int32, (CHUNK, CHUNK), 1)
    eye = jnp.where(r == c, 1.0, 0.0).astype(F32)
    s = 1
    inv = eye
    while s < CHUNK:
        sel = ((r // (2 * s)) == (c // (2 * s))) & ((r % (2 * s)) >= s) & ((c % (2 * s)) < s)
        low_s = jnp.where(sel, low, 0.0)
        if s == 1:
            inv = eye - low_s
        else:
            inv = inv - _dot(_bf(_dot(_bf(inv), _bf(low_s))), _bf(inv))
        s *= 2
    return inv


def _gdn_kernel(q_ref, k_ref, v_ref, z_ref, gates_ref, wq_ref, wk_ref, wv_ref, avec_ref, dtb_ref,
                nw_ref, o_ref, xbuf, st_ref, *, tb):
    tt = pl.program_id(2)
    kh = pl.program_id(1)
    halo = SUBLANES

    @pl.when(tt == 0)
    def _():
        st_ref[...] = jnp.zeros_like(st_ref)
        xbuf[0:halo, :] = jnp.zeros((halo, 4 * HEAD), F32)

    xbuf[halo:halo + tb, 0:HEAD] = q_ref[...]
    xbuf[halo:halo + tb, HEAD:2 * HEAD] = k_ref[...]
    xbuf[halo:halo + tb, 2 * HEAD:4 * HEAD] = v_ref[...]

    w_all = jnp.concatenate([wq_ref[...], wk_ref[...], wv_ref[...]], axis=1)
    conv = jnp.zeros((tb, 4 * HEAD), F32)
    for j in range(CONV_K):
        d = CONV_K - 1 - j
        conv = conv + xbuf[halo - d:halo - d + tb, :] * w_all[j:j + 1, :]
    xbuf[0:halo, :] = xbuf[tb:tb + halo, :]
    conv = _silu(conv)

    def l2n(x):
        return x * lax.rsqrt(jnp.sum(x * x, axis=-1, keepdims=True) + RMS_EPS)

    q_all = l2n(conv[:, 0:HEAD]) * (HEAD ** -0.5)
    k_all = l2n(conv[:, HEAD:2 * HEAD])

    gl = gates_ref[...]
    beta_all = jax.nn.sigmoid(gl)
    g_all = avec_ref[...] * _softplus(gl + dtb_ref[...])
    lane = lax.broadcasted_iota(jnp.int32, (tb, LANES), 1)

    r64 = lax.broadcasted_iota(jnp.int32, (CHUNK, CHUNK), 0)
    c64 = lax.broadcasted_iota(jnp.int32, (CHUNK, CHUNK), 1)
    tri = jnp.where(r64 >= c64, 1.0, 0.0).astype(F32)
    causal = r64 >= c64
    strict = r64 > c64
    avg = jnp.full((CHUNK, CHUNK), 1.0 / CHUNK, F32)

    for hv in range(2):
        head = 2 * kh + hv
        beta_col = jnp.sum(jnp.where(lane == head, beta_all, 0.0), axis=-1, keepdims=True)
        g_col = jnp.sum(jnp.where(lane == N_V_HEADS + head, g_all, 0.0), axis=-1, keepdims=True)
        v_all = conv[:, (2 + hv) * HEAD:(3 + hv) * HEAD]
        state = st_ref[hv]
        for c in range(tb // CHUNK):
            rows = slice(c * CHUNK, (c + 1) * CHUNK)
            q = q_all[rows, :]
            k = k_all[rows, :]
            v = v_all[rows, :]
            beta = beta_col[rows, :]
            gcum = jnp.dot(tri, jnp.broadcast_to(g_col[rows, :], (CHUNK, CHUNK)),
                           precision=HIGHEST, preferred_element_type=F32)
            gcum_t = _dot_nt(avg, gcum, precision=HIGHEST)
            decay = jnp.where(causal, jnp.exp(jnp.minimum(gcum - gcum_t, 0.0)), 0.0)
            g_i = gcum[:, 0:1]
            g_last = gcum[CHUNK - 1:CHUNK, 0:1]
            kb = k * beta
            low = jnp.where(strict, _dot_nt(_bf(kb), _bf(k)) * decay, 0.0)
            inv = _unit_lower_inverse(low)
            rhs = jnp.concatenate([v * beta, kb * jnp.exp(g_i)], axis=1)
            sol = _dot(_bf(inv), _bf(rhs))
            u = sol[:, :HEAD]
            w = sol[:, HEAD:]
            attn = _dot_nt(_bf(q), _bf(k)) * decay
            q_dec = q * jnp.exp(g_i)
            k_tail = k * jnp.exp(g_last - g_i)

            sb = _bf(state)
            v_new = u - _dot(_bf(w), sb)
            o = _dot(_bf(q_dec), sb) + _dot(_bf(attn), _bf(v_new))
            state = state * jnp.exp(g_last) + _dot_tn(_bf(k_tail), _bf(v_new))

            ms = jnp.mean(o * o, axis=-1, keepdims=True)
            o = o * lax.rsqrt(ms + RMS_EPS) * nw_ref[...]
            o_ref[rows, hv * HEAD:(hv + 1) * HEAD] = _bf(o * _silu(z_ref[rows, hv * HEAD:(hv + 1) * HEAD]))
        st_ref[hv] = state


def _gdn(proj, conv_w, a_vec, dtb_vec, norm_w, tb):
    bsz, s, _ = proj.shape
    k_heads = D_MODEL // HEAD
    gate_block = ODD_MAIN // LANES

    in_specs = [
        pl.BlockSpec((None, tb, HEAD), lambda b, h, t: (b, t, h)),
        pl.BlockSpec((None, tb, HEAD), lambda b, h, t: (b, t, k_heads + h)),
        pl.BlockSpec((None, tb, 2 * HEAD), lambda b, h, t: (b, t, k_heads + h)),
        pl.BlockSpec((None, tb, 2 * HEAD), lambda b, h, t: (b, t, 2 * k_heads + h)),
        pl.BlockSpec((None, tb, LANES), lambda b, h, t: (b, t, gate_block)),
        pl.BlockSpec((CONV_K, HEAD), lambda b, h, t: (0, h)),
        pl.BlockSpec((CONV_K, HEAD), lambda b, h, t: (0, k_heads + h)),
        pl.BlockSpec((CONV_K, 2 * HEAD), lambda b, h, t: (0, k_heads + h)),
        pl.BlockSpec((1, LANES), lambda b, h, t: (0, 0)),
        pl.BlockSpec((1, LANES), lambda b, h, t: (0, 0)),
        pl.BlockSpec((1, HEAD), lambda b, h, t: (0, 0)),
    ]
    return pl.pallas_call(
        functools.partial(_gdn_kernel, tb=tb),
        grid=(bsz, k_heads, s // tb),
        in_specs=in_specs,
        out_specs=pl.BlockSpec((None, tb, 2 * HEAD), lambda b, h, t: (b, t, h)),
        out_shape=jax.ShapeDtypeStruct((bsz, s, 2 * D_MODEL), BF16),
        scratch_shapes=[pltpu.VMEM((tb + SUBLANES, 4 * HEAD), F32),
                        pltpu.VMEM((2, HEAD, HEAD), F32)],
        compiler_params=_params("parallel", "parallel", "arbitrary"),
        name="gated_deltanet",
    )(proj, proj, proj, proj, proj, conv_w, conv_w, conv_w, a_vec, dtb_vec, norm_w.reshape(1, HEAD))


def kernel(x, norm_w, final_norm_w, even_w_in, even_w_out, hgrn_lb_logits, hgrn_norm_w,
           odd_w_in, odd_conv_w, odd_dt_bias, odd_a_log, odd_norm_w, odd_w_out):
    bsz, s, d = x.shape
    depth = norm_w.shape[0]
    t = bsz * s
    tm = 512
    tb = 256

    lb_all = jnp.cumsum(jax.nn.softmax(hgrn_lb_logits.astype(F32), axis=0), axis=0)
    lb_all = jnp.maximum(lb_all - lb_all[0:1], 0.0)
    log_lb = jnp.log(lb_all)
    log_1mlb = jnp.log1p(-lb_all)

    h = x.reshape(t, d)
    for layer in range(depth):
        j = layer // 2
        final_w = final_norm_w if layer == depth - 1 else None
        if layer % 2 == 0:
            proj = _norm_matmul(h, norm_w[layer], _bf(even_w_in[j]), tm, 1024)
            proj = proj.reshape(bsz, s, EVEN_IN)
            a_mix = _dilated_attention(proj)
            b_mix = _hgrn(proj, log_lb[j], log_1mlb[j], hgrn_norm_w[j], tb)
            w_out = _bf(even_w_out[j])
            h = _out_proj(h, [a_mix.reshape(t, d), b_mix.reshape(t, d)], [w_out[:d], w_out[d:]],
                          final_w, tm)
        else:
            pad = jnp.zeros((d, ODD_IN_PAD - odd_w_in.shape[2]), F32)
            w_in = _bf(jnp.concatenate([odd_w_in[j], pad], axis=1))
            proj = _norm_matmul(h, norm_w[layer], w_in, tm, ODD_IN_PAD // 7)
            proj = proj.reshape(bsz, s, ODD_IN_PAD)
            zeros16 = jnp.zeros((N_V_HEADS,), F32)
            tail = jnp.zeros((LANES - 2 * N_V_HEADS,), F32)
            a_vec = jnp.concatenate([zeros16, -jnp.exp(odd_a_log[j].astype(F32)), tail]).reshape(1, LANES)
            dtb_vec = jnp.concatenate([zeros16, odd_dt_bias[j].astype(F32), tail]).reshape(1, LANES)
            o = _gdn(proj, odd_conv_w[j], a_vec, dtb_vec, odd_norm_w[j], tb)
            h = _out_proj(h, [o.reshape(t, 2 * d)], [_bf(odd_w_out[j])], final_w, tm)
    return h.reshape(bsz, s, d)
```

```python
import functools

import jax
import jax.numpy as jnp
from jax import lax
from jax.experimental import pallas as pl
from jax.experimental.pallas import tpu as pltpu

F32 = jnp.float32
BF16 = jnp.bfloat16
HIGHEST = lax.Precision.HIGHEST

RMS_EPS = 1e-6
LANES = 128
SUBLANES = 8
VMEM_LIMIT = 48 * 1024 * 1024

D_MODEL = 1024
A_HEAD_DIM = 64
A_BLOCK = 128
A_DILATIONS = (1, 4, 16)
CHUNK = 64
A_MOD = 16
A_GROUP = 4
HEAD = 128
CONV_K = 4

EVEN_IN = 8 * D_MODEL
ODD_MAIN = 6 * D_MODEL
ODD_IN_PAD = ODD_MAIN + LANES
N_V_HEADS = 16


def _dot(a, b):
    return jnp.dot(a, b, preferred_element_type=F32)


def _dot_nt(a, b, precision=None):
    return lax.dot_general(a, b, (((1,), (1,)), ((), ())), precision=precision,
                           preferred_element_type=F32)


def _dot_tn(a, b):
    return lax.dot_general(a, b, (((0,), (0,)), ((), ())), preferred_element_type=F32)


def _bf(x):
    return x.astype(BF16)


def _silu(x):
    return x * jax.nn.sigmoid(x)


def _softplus(x):
    return jnp.maximum(x, 0.0) + jnp.log1p(jnp.exp(-jnp.abs(x)))


def _params(*sem):
    return pltpu.CompilerParams(dimension_semantics=sem, vmem_limit_bytes=VMEM_LIMIT)


def _norm_matmul_kernel(x_ref, nw_ref, w_ref, o_ref, xn_ref):
    @pl.when(pl.program_id(1) == 0)
    def _():
        x = x_ref[...]
        ms = jnp.mean(x * x, axis=-1, keepdims=True)
        xn_ref[...] = _bf(x * lax.rsqrt(ms + RMS_EPS) * nw_ref[...])

    o_ref[...] = _dot(xn_ref[...], w_ref[...])


def _norm_matmul(x, nw, w, tm, tn):
    t, d = x.shape
    n = w.shape[1]
    return pl.pallas_call(
        _norm_matmul_kernel,
        grid=(t // tm, n // tn),
        in_specs=[pl.BlockSpec((tm, d), lambda i, j: (i, 0)),
                  pl.BlockSpec((1, d), lambda i, j: (0, 0)),
                  pl.BlockSpec((d, tn), lambda i, j: (0, j))],
        out_specs=pl.BlockSpec((tm, tn), lambda i, j: (i, j)),
        out_shape=jax.ShapeDtypeStruct((t, n), F32),
        scratch_shapes=[pltpu.VMEM((tm, d), BF16)],
        compiler_params=_params("parallel", "arbitrary"),
        name="norm_in_proj",
    )(x, nw.reshape(1, d), w)


def _out_proj_kernel(*refs, n_in, final):
    h_ref = refs[0]
    a_refs = refs[1:1 + n_in]
    w_refs = refs[1 + n_in:1 + 2 * n_in]
    o_ref = refs[-1]
    acc = h_ref[...]
    for a_ref, w_ref in zip(a_refs, w_refs):
        acc = acc + _dot(a_ref[...], w_ref[...])
    if final:
        fw_ref = refs[1 + 2 * n_in]
        ms = jnp.mean(acc * acc, axis=-1, keepdims=True)
        acc = acc * lax.rsqrt(ms + RMS_EPS) * fw_ref[...]
    o_ref[...] = acc


def _out_proj(h, acts, ws, final_w, tm):
    t, d = h.shape
    n_in = len(acts)
    final = final_w is not None
    in_specs = [pl.BlockSpec((tm, d), lambda i: (i, 0))]
    in_specs += [pl.BlockSpec((tm, a.shape[1]), lambda i: (i, 0)) for a in acts]
    in_specs += [pl.BlockSpec(w.shape, lambda i: (0, 0)) for w in ws]
    args = [h, *acts, *ws]
    if final:
        in_specs.append(pl.BlockSpec((1, d), lambda i: (0, 0)))
        args.append(final_w.reshape(1, d))
    return pl.pallas_call(
        functools.partial(_out_proj_kernel, n_in=n_in, final=final),
        grid=(t // tm,),
        in_specs=in_specs,
        out_specs=pl.BlockSpec((tm, d), lambda i: (i, 0)),
        out_shape=jax.ShapeDtypeStruct((t, d), F32),
        compiler_params=_params("parallel"),
        name="out_proj",
    )(*args)


def _attn_kernel(q_ref, k_ref, v_ref, gate_ref, o_ref, qp_ref, kp_ref, vp_ref, acc_ref, m_ref, l_ref,
                 *, seq):
    per_res = seq // A_MOD
    n_blocks = seq // A_BLOCK
    row = lax.broadcasted_iota(jnp.int32, (A_BLOCK, 2 * A_BLOCK), 0)
    col = lax.broadcasted_iota(jnp.int32, (A_BLOCK, 2 * A_BLOCK), 1)
    lane = lax.broadcasted_iota(jnp.int32, (A_BLOCK, LANES), 1)
    head0 = lane < A_HEAD_DIM

    def to_residue_major(res, carry):
        src = pl.ds(res, per_res, stride=A_MOD)
        dst = pl.ds(pl.multiple_of(res * per_res, per_res), per_res)
        qp_ref[dst, :] = q_ref[src, :]
        kp_ref[dst, :] = k_ref[src, :]
        vp_ref[dst, :] = v_ref[src, :]
        return carry

    lax.fori_loop(0, A_MOD, to_residue_major, 0)

    for dilation in A_DILATIONS:
        first = dilation == A_DILATIONS[0]
        per_class = n_blocks // dilation
        runs = A_MOD // dilation
        rq = A_BLOCK // runs
        dist = runs * (row % rq - col % (2 * rq)) + (row // rq - col // (2 * rq)) + A_BLOCK
        band = (dist >= 0) & (dist <= A_BLOCK)
        in_cur = col % (2 * rq) >= rq

        def load(n, dilation=dilation, first=first, per_class=per_class, runs=runs, rq=rq):
            res = n // per_class
            i = n % per_class
            cur, prev = [], []
            for c in range(runs):
                base = pl.multiple_of((c * dilation + res) * per_res + rq * i, SUBLANES)
                cur.append(pl.ds(base, rq))
                prev.append(pl.ds(pl.multiple_of(jnp.where(i > 0, base - rq, base), SUBLANES), rq))

            def window(ref):
                return _bf(jnp.concatenate([ref[rows, :] for pc in zip(prev, cur) for rows in pc], axis=0))

            def block(ref):
                return jnp.concatenate([ref[rows, :] for rows in cur], axis=0)

            old = None if first else (block(acc_ref), block(m_ref), block(l_ref))
            return cur, i, block(qp_ref), window(kp_ref), window(vp_ref), old

        def update(i, q, kw, vw, old, band=band, in_cur=in_cur):
            mask = band & (in_cur | (i > 0))
            parts = []
            for h in range(2):
                qh = _bf(jnp.where(head0 if h == 0 else ~head0, q, 0.0))
                s = _dot_nt(qh, kw) * (A_HEAD_DIM ** -0.5)
                s = jnp.where(mask, s, -jnp.inf)
                if old is None:
                    m_new = jnp.max(s, axis=-1, keepdims=True)
                else:
                    m_old = jnp.where(head0 if h == 0 else ~head0, old[1], -jnp.inf)
                    m_new = jnp.max(jnp.maximum(jnp.maximum(s[:, :LANES], m_old), s[:, LANES:]),
                                    axis=-1, keepdims=True)
                p = jnp.exp(s - m_new)
                parts.append((m_new, jnp.sum(p, axis=-1, keepdims=True), _dot(_bf(p), vw)))
            m_new = jnp.where(head0, parts[0][0], parts[1][0])
            l_new = jnp.where(head0, parts[0][1], parts[1][1])
            acc = jnp.where(head0, parts[0][2], parts[1][2])
            if old is not None:
                alpha = jnp.exp(old[1] - m_new)
                acc = old[0] * alpha + acc
                l_new = old[2] * alpha + l_new
            return acc, m_new, l_new

        def group(t, carry, load=load, update=update, rq=rq):
            loaded = [load(t * A_GROUP + u) for u in range(A_GROUP)]
            results = [update(*item[1:]) for item in loaded]
            for (cur, *_), result in zip(loaded, results):
                for ref, val in zip((acc_ref, m_ref, l_ref), result):
                    for c, rows in enumerate(cur):
                        ref[rows, :] = val[c * rq:(c + 1) * rq, :]
            return carry

        lax.fori_loop(0, n_blocks // A_GROUP, group, 0)

    def to_position_order(res, carry):
        src = pl.ds(pl.multiple_of(res * per_res, per_res), per_res)
        m_ref[pl.ds(res, per_res, stride=A_MOD), :] = acc_ref[src, :] / l_ref[src, :]
        return carry

    lax.fori_loop(0, A_MOD, to_position_order, 0)

    def finish(t, carry):
        rows = pl.ds(pl.multiple_of(t * A_BLOCK, A_BLOCK), A_BLOCK)
        o_ref[rows, :] = _bf(m_ref[rows, :] * _silu(gate_ref[rows, :]))
        return carry

    lax.fori_loop(0, n_blocks, finish, 0)


def _dilated_attention(proj):
    bsz, s, _ = proj.shape
    hp_blocks = D_MODEL // LANES

    def col_spec(off):
        return pl.BlockSpec((None, s, LANES), lambda b, hp: (b, 0, off * hp_blocks + hp))

    return pl.pallas_call(
        functools.partial(_attn_kernel, seq=s),
        grid=(bsz, hp_blocks),
        in_specs=[col_spec(0), col_spec(1), col_spec(2), col_spec(3)],
        out_specs=pl.BlockSpec((None, s, LANES), lambda b, hp: (b, 0, hp)),
        out_shape=jax.ShapeDtypeStruct((bsz, s, D_MODEL), BF16),
        scratch_shapes=[pltpu.VMEM((s, LANES), F32)] * 6,
        compiler_params=_params("parallel", "parallel"),
        name="dilated_attn",
    )(proj, proj, proj, proj)


def _hgrn_kernel(q_ref, f_ref, i_ref, gate_ref, loglb_ref, log1mlb_ref, nw_ref, o_ref, st_ref, *, tb):
    @pl.when(pl.program_id(2) == 0)
    def _():
        st_ref[...] = jnp.zeros_like(st_ref)

    n_ch = tb // CHUNK
    r64 = lax.broadcasted_iota(jnp.int32, (CHUNK, CHUNK), 0)
    c64 = lax.broadcasted_iota(jnp.int32, (CHUNK, CHUNK), 1)
    tri = _bf(jnp.where(r64 >= c64, 1.0, 0.0))
    row = lax.broadcasted_iota(jnp.int32, (CHUNK, HEAD), 0)
    sub = lax.broadcasted_iota(jnp.int32, (SUBLANES, HEAD), 0)
    log_lb = loglb_ref[...]
    log_1mlb = log1mlb_ref[...]

    x = f_ref[...]
    log_sig = jnp.minimum(x, 0.0) - jnp.log1p(jnp.exp(-jnp.abs(x)))
    b = log_1mlb + log_sig
    log_f_all = jnp.maximum(log_lb, b) + jnp.log1p(jnp.exp(-jnp.abs(log_lb - b)))
    k_all = jnp.exp(log_1mlb + log_sig - x)

    chunk_rows = [slice(c * CHUNK, (c + 1) * CHUNK) for c in range(n_ch)]
    qs = [q_ref[rows, :] for rows in chunk_rows]
    ks = [k_all[rows, :] for rows in chunk_rows]
    vbs = [_bf(i_ref[rows, :]) for rows in chunk_rows]

    gs = []
    for rows in chunk_rows:
        lf = log_f_all[rows, :]
        hi = _bf(lf)
        r1 = lf - hi.astype(F32)
        mid = _bf(r1)
        lo = _bf(r1 - mid.astype(F32))
        cum = _dot(tri, jnp.concatenate([hi, mid, lo], axis=1))
        gs.append(cum[:, :HEAD] + cum[:, HEAD:2 * HEAD] + cum[:, 2 * HEAD:])

    attn = [None] * n_ch
    s = CHUNK // 2
    while s >= 1:
        blk = 2 * s
        upper = (row % blk) >= s
        mask = ((r64 // blk) == (c64 // blk)) & ((r64 % blk) >= s) & ((c64 % blk) < s)
        for c in range(n_ch):
            g = gs[c]
            if blk >= SUBLANES:
                g_ref = jnp.concatenate(
                    [jnp.broadcast_to(g[b0 + s - 1:b0 + s, :], (blk, HEAD)) for b0 in range(0, CHUNK, blk)],
                    axis=0)
            elif s == 2:
                g_ref = jnp.concatenate(
                    [jnp.where(sub < 4, jnp.broadcast_to(g[v0 + 1:v0 + 2, :], (SUBLANES, HEAD)),
                               jnp.broadcast_to(g[v0 + 5:v0 + 6, :], (SUBLANES, HEAD)))
                     for v0 in range(0, CHUNK, SUBLANES)], axis=0)
            else:
                g_ref = jnp.where(upper, pltpu.roll(g, 1, 0), g)
            z = _bf(jnp.where(upper, qs[c], ks[c]) * jnp.exp(-jnp.abs(g - g_ref)))
            m = _dot_nt(z, z)
            attn[c] = jnp.where(mask, m, 0.0) if attn[c] is None else jnp.where(mask, m, attn[c])
        s //= 2
    eye = r64 == c64
    attn = [jnp.where(eye, jnp.sum(qs[c] * ks[c], axis=-1, keepdims=True), attn[c]) for c in range(n_ch)]

    o_intra = [_dot(_bf(attn[c]), vbs[c]) for c in range(n_ch)]
    q_dec = [_bf(qs[c] * jnp.exp(gs[c])) for c in range(n_ch)]
    g_last = [gs[c][CHUNK - 1:CHUNK, :] for c in range(n_ch)]
    kv = [_dot_tn(vbs[c], _bf(ks[c] * jnp.exp(g_last[c] - gs[c]))) for c in range(n_ch)]

    state = st_ref[...]
    outs = []
    for c in range(n_ch):
        outs.append(_dot_nt(q_dec[c], _bf(state)) + o_intra[c])
        state = state * jnp.exp(g_last[c]) + kv[c]
    st_ref[...] = state

    for c, rows in enumerate(chunk_rows):
        o = outs[c]
        ms = jnp.mean(o * o, axis=-1, keepdims=True)
        o = o * lax.rsqrt(ms + RMS_EPS) * nw_ref[...]
        o_ref[rows, :] = _bf(o * _silu(gate_ref[rows, :]))


def _hgrn(proj, log_lb, log_1mlb, norm_w, tb):
    bsz, s, _ = proj.shape
    heads = D_MODEL // HEAD
    base = 4 * heads

    def col_spec(off):
        return pl.BlockSpec((None, tb, HEAD), lambda b, h, t: (b, t, base + off * heads + h))

    vec_spec = pl.BlockSpec((None, 1, HEAD), lambda b, h, t: (h, 0, 0))
    return pl.pallas_call(
        functools.partial(_hgrn_kernel, tb=tb),
        grid=(bsz, heads, s // tb),
        in_specs=[col_spec(0), col_spec(1), col_spec(2), col_spec(3), vec_spec, vec_spec,
                  pl.BlockSpec((1, HEAD), lambda b, h, t: (0, 0))],
        out_specs=pl.BlockSpec((None, tb, HEAD), lambda b, h, t: (b, t, h)),
        out_shape=jax.ShapeDtypeStruct((bsz, s, D_MODEL), BF16),
        scratch_shapes=[pltpu.VMEM((HEAD, HEAD), F32)],
        compiler_params=_params("parallel", "parallel", "arbitrary"),
        name="hgrn2",
    )(proj, proj, proj, proj, log_lb.reshape(heads, 1, HEAD), log_1mlb.reshape(heads, 1, HEAD),
      norm_w.reshape(1, HEAD))


def _unit_lower_inverses(lows):
    r = lax.broadcasted_iota(jnp.int32, (CHUNK, CHUNK), 0)
    c = lax.broadcasted_iota(jnp.int32, (CHUNK, CHUNK), 1)
    eye = jnp.where(r == c, 1.0, 0.0).astype(F32)
    s = 1
    invs = None
    while s < CHUNK:
        sel = ((r // (2 * s)) == (c // (2 * s))) & ((r % (2 * s)) >= s) & ((c % (2 * s)) < s)
        low_s = [jnp.where(sel, low, 0.0) for low in lows]
        if s == 1:
            invs = [eye - x for x in low_s]
        else:
            inv_b = [_bf(x) for x in invs]
            tmp = [_bf(_dot(a, _bf(x))) for a, x in zip(inv_b, low_s)]
            invs = [x - _dot(t, a) for x, t, a in zip(invs, tmp, inv_b)]
        s *= 2
    return invs


def _gdn_kernel(q_ref, k_ref, v_ref, z_ref, gates_ref, wq_ref, wk_ref, wv_ref, avec_ref, dtb_ref,
                nw_ref, o_ref, xbuf, st_ref, *, tb):
    tt = pl.program_id(2)
    kh = pl.program_id(1)
    halo = SUBLANES
    n_ch = tb // CHUNK

    @pl.when(tt == 0)
    def _():
        st_ref[...] = jnp.zeros_like(st_ref)
        xbuf[0:halo, :] = jnp.zeros((halo, 4 * HEAD), F32)

    xbuf[halo:halo + tb, 0:HEAD] = q_ref[...]
    xbuf[halo:halo + tb, HEAD:2 * HEAD] = k_ref[...]
    xbuf[halo:halo + tb, 2 * HEAD:4 * HEAD] = v_ref[...]

    w_all = jnp.concatenate([wq_ref[...], wk_ref[...], wv_ref[...]], axis=1)
    conv = jnp.zeros((tb, 4 * HEAD), F32)
    for j in range(CONV_K):
        d = CONV_K - 1 - j
        conv = conv + xbuf[halo - d:halo - d + tb, :] * w_all[j:j + 1, :]
    xbuf[0:halo, :] = xbuf[tb:tb + halo, :]
    conv = _silu(conv)

    def l2n(x):
        return x * lax.rsqrt(jnp.sum(x * x, axis=-1, keepdims=True) + RMS_EPS)

    q_all = l2n(conv[:, 0:HEAD]) * (HEAD ** -0.5)
    k_all = l2n(conv[:, HEAD:2 * HEAD])

    gl = gates_ref[...]
    beta_all = jax.nn.sigmoid(gl)
    g_all = avec_ref[...] * _softplus(gl + dtb_ref[...])
    lane = lax.broadcasted_iota(jnp.int32, (tb, LANES), 1)

    r64 = lax.broadcasted_iota(jnp.int32, (CHUNK, CHUNK), 0)
    c64 = lax.broadcasted_iota(jnp.int32, (CHUNK, CHUNK), 1)
    tri = _bf(jnp.where(r64 >= c64, 1.0, 0.0))
    causal = r64 >= c64
    strict = r64 > c64
    strict_f = jnp.where(strict, 1.0, 0.0).astype(F32)

    chunk_rows = [slice(c * CHUNK, (c + 1) * CHUNK) for c in range(n_ch)]
    units = [(hv, c) for c in range(n_ch) for hv in range(2)]

    beta_cols, g_cols = [], []
    for hv in range(2):
        head = 2 * kh + hv
        beta_cols.append(jnp.sum(jnp.where(lane == head, beta_all, 0.0), axis=-1, keepdims=True))
        g_cols.append(jnp.sum(jnp.where(lane == N_V_HEADS + head, g_all, 0.0), axis=-1, keepdims=True))

    k_bf = [_bf(k_all[rows, :]) for rows in chunk_rows]
    kq = [_dot_nt(_bf(jnp.concatenate([k_all[rows, :], q_all[rows, :]], axis=0)), kb_)
          for rows, kb_ in zip(chunk_rows, k_bf)]

    decay, g_i, g_last = [], [], []
    for hv, c in units:
        gb = jnp.broadcast_to(g_cols[hv][chunk_rows[c], :], (CHUNK, CHUNK))
        wmat = jnp.concatenate([gb * strict_f, gb], axis=1)
        hi = _bf(wmat)
        lo = _bf(wmat - hi.astype(F32))
        cum = _dot(tri, hi) + _dot(tri, lo)
        decay.append(jnp.where(causal, jnp.exp(jnp.minimum(cum[:, :CHUNK], 0.0)), 0.0))
        g_i.append(cum[:, CHUNK:CHUNK + 1])
        g_last.append(cum[CHUNK - 1:CHUNK, CHUNK:CHUNK + 1])

    lows = [jnp.where(strict, kq[c][:CHUNK, :] * beta_cols[hv][chunk_rows[c], :] * decay[n], 0.0)
            for n, (hv, c) in enumerate(units)]
    invs = _unit_lower_inverses(lows)

    sols = []
    for n, (hv, c) in enumerate(units):
        rows = chunk_rows[c]
        beta = beta_cols[hv][rows, :]
        v = conv[rows, (2 + hv) * HEAD:(3 + hv) * HEAD]
        rhs = jnp.concatenate([v * beta, k_all[rows, :] * (beta * jnp.exp(g_i[n]))], axis=1)
        sols.append(_bf(_dot(_bf(invs[n]), _bf(rhs))))

    o_loc, q_eff, c_mat, p_mat = [], [], [], []
    for n, (hv, c) in enumerate(units):
        rows = chunk_rows[c]
        attn = _bf(kq[c][CHUNK:, :] * decay[n])
        auw = _dot(attn, sols[n])
        o_loc.append(auw[:, :HEAD])
        q_eff.append(_bf(q_all[rows, :] * jnp.exp(g_i[n]) - auw[:, HEAD:]))
        k_tail = _bf(k_all[rows, :] * jnp.exp(g_last[n] - g_i[n]))
        ktuw = _dot_tn(k_tail, sols[n])
        c_mat.append(ktuw[:, :HEAD])
        p_mat.append(_bf(ktuw[:, HEAD:]))

    states = [st_ref[0], st_ref[1]]
    outs = [None] * len(units)
    for n, (hv, c) in enumerate(units):
        sb = _bf(states[hv])
        outs[n] = _dot(q_eff[n], sb) + o_loc[n]
        states[hv] = states[hv] * jnp.exp(g_last[n]) + c_mat[n] - _dot(p_mat[n], sb)
    st_ref[0] = states[0]
    st_ref[1] = states[1]

    for n, (hv, c) in enumerate(units):
        rows = chunk_rows[c]
        o = outs[n]
        ms = jnp.mean(o * o, axis=-1, keepdims=True)
        o = o * lax.rsqrt(ms + RMS_EPS) * nw_ref[...]
        o_ref[rows, hv * HEAD:(hv + 1) * HEAD] = _bf(o * _silu(z_ref[rows, hv * HEAD:(hv + 1) * HEAD]))


def _gdn(proj, conv_w, a_vec, dtb_vec, norm_w, tb):
    bsz, s, _ = proj.shape
    k_heads = D_MODEL // HEAD
    gate_block = ODD_MAIN // LANES

    in_specs = [
        pl.BlockSpec((None, tb, HEAD), lambda b, h, t: (b, t, h)),
        pl.BlockSpec((None, tb, HEAD), lambda b, h, t: (b, t, k_heads + h)),
        pl.BlockSpec((None, tb, 2 * HEAD), lambda b, h, t: (b, t, k_heads + h)),
        pl.BlockSpec((None, tb, 2 * HEAD), lambda b, h, t: (b, t, 2 * k_heads + h)),
        pl.BlockSpec((None, tb, LANES), lambda b, h, t: (b, t, gate_block)),
        pl.BlockSpec((CONV_K, HEAD), lambda b, h, t: (0, h)),
        pl.BlockSpec((CONV_K, HEAD), lambda b, h, t: (0, k_heads + h)),
        pl.BlockSpec((CONV_K, 2 * HEAD), lambda b, h, t: (0, k_heads + h)),
        pl.BlockSpec((1, LANES), lambda b, h, t: (0, 0)),
        pl.BlockSpec((1, LANES), lambda b, h, t: (0, 0)),
        pl.BlockSpec((1, HEAD), lambda b, h, t: (0, 0)),
    ]
    return pl.pallas_call(
        functools.partial(_gdn_kernel, tb=tb),
        grid=(bsz, k_heads, s // tb),
        in_specs=in_specs,
        out_specs=pl.BlockSpec((None, tb, 2 * HEAD), lambda b, h, t: (b, t, h)),
        out_shape=jax.ShapeDtypeStruct((bsz, s, 2 * D_MODEL), BF16),
        scratch_shapes=[pltpu.VMEM((tb + SUBLANES, 4 * HEAD), F32),
                        pltpu.VMEM((2, HEAD, HEAD), F32)],
        compiler_params=_params("parallel", "parallel", "arbitrary"),
        name="gated_deltanet",
    )(proj, proj, proj, proj, proj, conv_w, conv_w, conv_w, a_vec, dtb_vec, norm_w.reshape(1, HEAD))


def kernel(x, norm_w, final_norm_w, even_w_in, even_w_out, hgrn_lb_logits, hgrn_norm_w,
           odd_w_in, odd_conv_w, odd_dt_bias, odd_a_log, odd_norm_w, odd_w_out):
    bsz, s, d = x.shape
    depth = norm_w.shape[0]
    t = bsz * s
    tm = 512
    tb = 256

    lb_all = jnp.cumsum(jax.nn.softmax(hgrn_lb_logits.astype(F32), axis=0), axis=0)
    lb_all = jnp.maximum(lb_all - lb_all[0:1], 0.0)
    log_lb = jnp.log(lb_all)
    log_1mlb = jnp.log1p(-lb_all)

    h = x.reshape(t, d)
    for layer in range(depth):
        j = layer // 2
        final_w = final_norm_w if layer == depth - 1 else None
        if layer % 2 == 0:
            proj = _norm_matmul(h, norm_w[layer], _bf(even_w_in[j]), tm, 1024)
            proj = proj.reshape(bsz, s, EVEN_IN)
            a_mix = _dilated_attention(proj)
            b_mix = _hgrn(proj, log_lb[j], log_1mlb[j], hgrn_norm_w[j], tb)
            w_out = _bf(even_w_out[j])
            h = _out_proj(h, [a_mix.reshape(t, d), b_mix.reshape(t, d)], [w_out[:d], w_out[d:]],
                          final_w, tm)
        else:
            pad = jnp.zeros((d, ODD_IN_PAD - odd_w_in.shape[2]), F32)
            w_in = _bf(jnp.concatenate([odd_w_in[j], pad], axis=1))
            proj = _norm_matmul(h, norm_w[layer], w_in, tm, ODD_IN_PAD // 7)
            proj = proj.reshape(bsz, s, ODD_IN_PAD)
            zeros16 = jnp.zeros((N_V_HEADS,), F32)
            tail = jnp.zeros((LANES - 2 * N_V_HEADS,), F32)
            a_vec = jnp.concatenate([zeros16, -jnp.exp(odd_a_log[j].astype(F32)), tail]).reshape(1, LANES)
            dtb_vec = jnp.concatenate([zeros16, odd_dt_bias[j].astype(F32), tail]).reshape(1, LANES)
            o = _gdn(proj, odd_conv_w[j], a_vec, dtb_vec, odd_norm_w[j], tb)
            h = _out_proj(h, [o.reshape(t, 2 * d)], [_bf(odd_w_out[j])], final_w, tm)
    return h.reshape(bsz, s, d)
```

```python
import functools

import jax
import jax.numpy as jnp
from jax import lax
from jax.experimental import pallas as pl
from jax.experimental.pallas import tpu as pltpu

F32 = jnp.float32
BF16 = jnp.bfloat16
HIGHEST = lax.Precision.HIGHEST

RMS_EPS = 1e-6
LANES = 128
SUBLANES = 8
VMEM_LIMIT = 48 * 1024 * 1024

D_MODEL = 1024
A_HEAD_DIM = 64
A_BLOCK = 128
A_DILATIONS = (1, 4, 16)
CHUNK = 64
A_SCALE = A_HEAD_DIM ** -0.5 * 1.4426950408889634
A_MOD = 16
A_GROUP = 8
HEAD = 128
CONV_K = 4

EVEN_IN = 8 * D_MODEL
ODD_MAIN = 6 * D_MODEL
ODD_IN_PAD = ODD_MAIN + LANES
N_V_HEADS = 16


def _dot(a, b):
    return jnp.dot(a, b, preferred_element_type=F32)


def _dot_nt(a, b, precision=None):
    return lax.dot_general(a, b, (((1,), (1,)), ((), ())), precision=precision,
                           preferred_element_type=F32)


def _dot_tn(a, b):
    return lax.dot_general(a, b, (((0,), (0,)), ((), ())), preferred_element_type=F32)


def _bf(x):
    return x.astype(BF16)


def _silu(x):
    return x * jax.nn.sigmoid(x)


def _softplus(x):
    return jnp.maximum(x, 0.0) + jnp.log1p(jnp.exp(-jnp.abs(x)))


def _params(*sem):
    return pltpu.CompilerParams(dimension_semantics=sem, vmem_limit_bytes=VMEM_LIMIT)


def _norm_matmul_kernel(x_ref, nw_ref, w_ref, o_ref, side_ref, xn_ref, *, side_tile, side_lo):
    @pl.when(pl.program_id(1) == 0)
    def _():
        x = x_ref[...]
        ms = jnp.mean(x * x, axis=-1, keepdims=True)
        xn_ref[...] = _bf(x * lax.rsqrt(ms + RMS_EPS) * nw_ref[...])

    acc = _dot(xn_ref[...], w_ref[...])
    o_ref[...] = _bf(acc)

    @pl.when(pl.program_id(1) == side_tile)
    def _():
        side_ref[...] = acc[:, side_lo:side_lo + side_ref.shape[1]]


def _norm_matmul(x, nw, w, tm, tn, side_start, side_width):
    t, d = x.shape
    n = w.shape[1]
    side_tile, side_lo = divmod(side_start, tn)
    assert side_lo + side_width <= tn
    return pl.pallas_call(
        functools.partial(_norm_matmul_kernel, side_tile=side_tile, side_lo=side_lo),
        grid=(t // tm, n // tn),
        in_specs=[pl.BlockSpec((tm, d), lambda i, j: (i, 0)),
                  pl.BlockSpec((1, d), lambda i, j: (0, 0)),
                  pl.BlockSpec((d, tn), lambda i, j: (0, j))],
        out_specs=[pl.BlockSpec((tm, tn), lambda i, j: (i, j)),
                   pl.BlockSpec((tm, side_width), lambda i, j: (i, 0))],
        out_shape=[jax.ShapeDtypeStruct((t, n), BF16),
                   jax.ShapeDtypeStruct((t, side_width), F32)],
        scratch_shapes=[pltpu.VMEM((tm, d), BF16)],
        compiler_params=_params("parallel", "arbitrary"),
        name="norm_in_proj",
    )(x, nw.reshape(1, d), w)


def _out_proj_kernel(*refs, n_in, final):
    h_ref = refs[0]
    a_refs = refs[1:1 + n_in]
    w_refs = refs[1 + n_in:1 + 2 * n_in]
    o_ref = refs[-1]
    acc = h_ref[...]
    for a_ref, w_ref in zip(a_refs, w_refs):
        acc = acc + _dot(a_ref[...], w_ref[...])
    if final:
        fw_ref = refs[1 + 2 * n_in]
        ms = jnp.mean(acc * acc, axis=-1, keepdims=True)
        acc = acc * lax.rsqrt(ms + RMS_EPS) * fw_ref[...]
    o_ref[...] = acc


def _out_proj(h, acts, ws, final_w, tm):
    t, d = h.shape
    n_in = len(acts)
    final = final_w is not None
    in_specs = [pl.BlockSpec((tm, d), lambda i: (i, 0))]
    in_specs += [pl.BlockSpec((tm, a.shape[1]), lambda i: (i, 0)) for a in acts]
    in_specs += [pl.BlockSpec(w.shape, lambda i: (0, 0)) for w in ws]
    args = [h, *acts, *ws]
    if final:
        in_specs.append(pl.BlockSpec((1, d), lambda i: (0, 0)))
        args.append(final_w.reshape(1, d))
    return pl.pallas_call(
        functools.partial(_out_proj_kernel, n_in=n_in, final=final),
        grid=(t // tm,),
        in_specs=in_specs,
        out_specs=pl.BlockSpec((tm, d), lambda i: (i, 0)),
        out_shape=jax.ShapeDtypeStruct((t, d), F32),
        compiler_params=_params("parallel"),
        name="out_proj",
    )(*args)


def _attn_kernel(q_ref, k_ref, v_ref, gate_ref, o_ref, qp_ref, kp_ref, vp_ref, acc_ref, m_ref, l_ref,
                 bias_ref, *, seq):
    per_res = seq // A_MOD
    n_blocks = seq // A_BLOCK
    row = lax.broadcasted_iota(jnp.int32, (A_BLOCK, 2 * A_BLOCK), 0)
    col = lax.broadcasted_iota(jnp.int32, (A_BLOCK, 2 * A_BLOCK), 1)
    lane = lax.broadcasted_iota(jnp.int32, (A_BLOCK, LANES), 1)
    head0 = lane < A_HEAD_DIM

    def widen(t, carry):
        rows = pl.ds(pl.multiple_of(t * A_BLOCK, A_BLOCK), A_BLOCK)
        acc_ref[rows, :] = q_ref[rows, :].astype(F32) * A_SCALE
        m_ref[rows, :] = k_ref[rows, :].astype(F32)
        l_ref[rows, :] = v_ref[rows, :].astype(F32)
        return carry

    lax.fori_loop(0, n_blocks, widen, 0)

    def to_residue_major(res, carry):
        src = pl.ds(res, per_res, stride=A_MOD)
        dst = pl.ds(pl.multiple_of(res * per_res, per_res), per_res)
        qp_ref[dst, :] = acc_ref[src, :]
        kp_ref[dst, :] = m_ref[src, :]
        vp_ref[dst, :] = l_ref[src, :]
        return carry

    lax.fori_loop(0, A_MOD, to_residue_major, 0)

    for dilation in A_DILATIONS:
        first = dilation == A_DILATIONS[0]
        per_class = n_blocks // dilation
        runs = A_MOD // dilation
        rq = A_BLOCK // runs
        dist = runs * (row % rq - col % (2 * rq)) + (row // rq - col // (2 * rq)) + A_BLOCK
        band = (dist >= 0) & (dist <= A_BLOCK)
        in_cur = col % (2 * rq) >= rq
        slot = 2 * A_DILATIONS.index(dilation)
        bias_ref[slot] = jnp.where(band & in_cur, 0.0, -jnp.inf)
        bias_ref[slot + 1] = jnp.where(band, 0.0, -jnp.inf)

        def load(n, dilation=dilation, first=first, per_class=per_class, runs=runs, rq=rq):
            res = n // per_class
            i = n % per_class
            cur, prev = [], []
            for c in range(runs):
                base = pl.multiple_of((c * dilation + res) * per_res + rq * i, SUBLANES)
                cur.append(pl.ds(base, rq))
                prev.append(pl.ds(pl.multiple_of(jnp.where(i > 0, base - rq, base), SUBLANES), rq))

            def window(ref):
                return _bf(jnp.concatenate([ref[rows, :] for pc in zip(prev, cur) for rows in pc], axis=0))

            def block(ref):
                return jnp.concatenate([ref[rows, :] for rows in cur], axis=0)

            return cur, i, block(qp_ref), window(kp_ref), window(vp_ref)

        def scores(i, q, kw, vw, slot=slot):
            bias = bias_ref[slot + (i > 0).astype(jnp.int32)]
            parts = []
            for h in range(2):
                qh = _bf(jnp.where(head0 if h == 0 else ~head0, q, 0.0))
                s = _dot_nt(qh, kw) + bias
                m_blk = jnp.max(s, axis=-1, keepdims=True)
                p = jnp.exp2(s - m_blk)
                parts.append((_dot(_bf(p), vw), m_blk, jnp.sum(p, axis=-1, keepdims=True)))
            return tuple(jnp.where(head0, a, b) for a, b in zip(*parts))

        def group(t, carry, load=load, scores=scores, first=first, rq=rq):
            loaded = [load(t * A_GROUP + u) for u in range(A_GROUP)]
            results = [scores(*item[1:]) for item in loaded]
            for (cur, *_), (acc, m_blk, l_blk) in zip(loaded, results):
                if not first:
                    acc_old, m_old, l_old = (
                        jnp.concatenate([ref[rows, :] for rows in cur], axis=0)
                        for ref in (acc_ref, m_ref, l_ref))
                    m_new = jnp.maximum(m_old, m_blk)
                    alpha = jnp.exp2(m_old - m_new)
                    beta = jnp.exp2(m_blk - m_new)
                    acc = acc_old * alpha + acc * beta
                    l_blk = l_old * alpha + l_blk * beta
                    m_blk = m_new
                for ref, val in zip((acc_ref, m_ref, l_ref), (acc, m_blk, l_blk)):
                    for c, rows in enumerate(cur):
                        ref[rows, :] = val[c * rq:(c + 1) * rq, :]
            return carry

        lax.fori_loop(0, n_blocks // A_GROUP, group, 0)

    def to_position_order(res, carry):
        src = pl.ds(pl.multiple_of(res * per_res, per_res), per_res)
        m_ref[pl.ds(res, per_res, stride=A_MOD), :] = acc_ref[src, :] / l_ref[src, :]
        return carry

    lax.fori_loop(0, A_MOD, to_position_order, 0)

    def finish(t, carry):
        rows = pl.ds(pl.multiple_of(t * A_BLOCK, A_BLOCK), A_BLOCK)
        o_ref[rows, :] = _bf(m_ref[rows, :] * _silu(gate_ref[rows, :].astype(F32)))
        return carry

    lax.fori_loop(0, n_blocks, finish, 0)


def _dilated_attention(proj):
    bsz, s, _ = proj.shape
    hp_blocks = D_MODEL // LANES

    def col_spec(off):
        return pl.BlockSpec((None, s, LANES), lambda b, hp: (b, 0, off * hp_blocks + hp))

    return pl.pallas_call(
        functools.partial(_attn_kernel, seq=s),
        grid=(bsz, hp_blocks),
        in_specs=[col_spec(0), col_spec(1), col_spec(2), col_spec(3)],
        out_specs=pl.BlockSpec((None, s, LANES), lambda b, hp: (b, 0, hp)),
        out_shape=jax.ShapeDtypeStruct((bsz, s, D_MODEL), BF16),
        scratch_shapes=[pltpu.VMEM((s, LANES), F32)] * 6
        + [pltpu.VMEM((2 * len(A_DILATIONS), A_BLOCK, 2 * A_BLOCK), F32)],
        compiler_params=_params("parallel", "parallel"),
        name="dilated_attn",
    )(proj, proj, proj, proj)


def _hgrn_kernel(q_ref, f_ref, i_ref, gate_ref, loglb_ref, log1mlb_ref, nw_ref, o_ref, st_ref, *, tb):
    @pl.when(pl.program_id(2) == 0)
    def _():
        st_ref[...] = jnp.zeros_like(st_ref)

    n_ch = tb // CHUNK
    r64 = lax.broadcasted_iota(jnp.int32, (CHUNK, CHUNK), 0)
    c64 = lax.broadcasted_iota(jnp.int32, (CHUNK, CHUNK), 1)
    tri = _bf(jnp.where(r64 >= c64, 1.0, 0.0))
    row = lax.broadcasted_iota(jnp.int32, (CHUNK, HEAD), 0)
    sub = lax.broadcasted_iota(jnp.int32, (SUBLANES, HEAD), 0)
    log_lb = loglb_ref[...]
    log_1mlb = log1mlb_ref[...]

    x = f_ref[...]
    log_sig = jnp.minimum(x, 0.0) - jnp.log1p(jnp.exp(-jnp.abs(x)))
    b = log_1mlb + log_sig
    log_f_all = jnp.maximum(log_lb, b) + jnp.log1p(jnp.exp(-jnp.abs(log_lb - b)))
    k_all = jnp.exp(log_1mlb + log_sig - x)

    chunk_rows = [slice(c * CHUNK, (c + 1) * CHUNK) for c in range(n_ch)]
    qs = [q_ref[rows, :].astype(F32) for rows in chunk_rows]
    ks = [k_all[rows, :] for rows in chunk_rows]
    vbs = [i_ref[rows, :] for rows in chunk_rows]

    gs = []
    for rows in chunk_rows:
        lf = log_f_all[rows, :]
        hi = _bf(lf)
        r1 = lf - hi.astype(F32)
        mid = _bf(r1)
        lo = _bf(r1 - mid.astype(F32))
        cum = _dot(tri, jnp.concatenate([hi, mid, lo], axis=1))
        gs.append(cum[:, :HEAD] + cum[:, HEAD:2 * HEAD] + cum[:, 2 * HEAD:])

    attn = [None] * n_ch
    s = CHUNK // 2
    while s >= 1:
        blk = 2 * s
        upper = (row % blk) >= s
        mask = ((r64 // blk) == (c64 // blk)) & ((r64 % blk) >= s) & ((c64 % blk) < s)
        for c in range(n_ch):
            g = gs[c]
            if blk >= SUBLANES:
                g_ref = jnp.concatenate(
                    [jnp.broadcast_to(g[b0 + s - 1:b0 + s, :], (blk, HEAD)) for b0 in range(0, CHUNK, blk)],
                    axis=0)
            elif s == 2:
                g_ref = jnp.concatenate(
                    [jnp.where(sub < 4, jnp.broadcast_to(g[v0 + 1:v0 + 2, :], (SUBLANES, HEAD)),
                               jnp.broadcast_to(g[v0 + 5:v0 + 6, :], (SUBLANES, HEAD)))
                     for v0 in range(0, CHUNK, SUBLANES)], axis=0)
            else:
                g_ref = jnp.where(upper, pltpu.roll(g, 1, 0), g)
            z = _bf(jnp.where(upper, qs[c], ks[c]) * jnp.exp(-jnp.abs(g - g_ref)))
            m = _dot_nt(z, z)
            attn[c] = jnp.where(mask, m, 0.0) if attn[c] is None else jnp.where(mask, m, attn[c])
        s //= 2
    eye = r64 == c64
    attn = [jnp.where(eye, jnp.sum(qs[c] * ks[c], axis=-1, keepdims=True), attn[c]) for c in range(n_ch)]

    o_intra = [_dot(_bf(attn[c]), vbs[c]) for c in range(n_ch)]
    q_dec = [_bf(qs[c] * jnp.exp(gs[c])) for c in range(n_ch)]
    g_last = [gs[c][CHUNK - 1:CHUNK, :] for c in range(n_ch)]
    kv = [_dot_tn(vbs[c], _bf(ks[c] * jnp.exp(g_last[c] - gs[c]))) for c in range(n_ch)]

    state = st_ref[...]
    outs = []
    for c in range(n_ch):
        outs.append(_dot_nt(q_dec[c], _bf(state)) + o_intra[c])
        state = state * jnp.exp(g_last[c]) + kv[c]
    st_ref[...] = state

    for c, rows in enumerate(chunk_rows):
        o = outs[c]
        ms = jnp.mean(o * o, axis=-1, keepdims=True)
        o = o * lax.rsqrt(ms + RMS_EPS) * nw_ref[...]
        o_ref[rows, :] = _bf(o * _silu(gate_ref[rows, :].astype(F32)))


def _hgrn(proj, f_logit, log_lb, log_1mlb, norm_w, tb):
    bsz, s, _ = proj.shape
    heads = D_MODEL // HEAD
    base = 4 * heads

    def col_spec(off):
        return pl.BlockSpec((None, tb, HEAD), lambda b, h, t: (b, t, base + off * heads + h))

    vec_spec = pl.BlockSpec((None, 1, HEAD), lambda b, h, t: (h, 0, 0))
    return pl.pallas_call(
        functools.partial(_hgrn_kernel, tb=tb),
        grid=(bsz, heads, s // tb),
        in_specs=[col_spec(0), pl.BlockSpec((None, tb, HEAD), lambda b, h, t: (b, t, h)),
                  col_spec(2), col_spec(3), vec_spec, vec_spec,
                  pl.BlockSpec((1, HEAD), lambda b, h, t: (0, 0))],
        out_specs=pl.BlockSpec((None, tb, HEAD), lambda b, h, t: (b, t, h)),
        out_shape=jax.ShapeDtypeStruct((bsz, s, D_MODEL), BF16),
        scratch_shapes=[pltpu.VMEM((HEAD, HEAD), F32)],
        compiler_params=_params("parallel", "parallel", "arbitrary"),
        name="hgrn2",
    )(proj, f_logit, proj, proj, log_lb.reshape(heads, 1, HEAD), log_1mlb.reshape(heads, 1, HEAD),
      norm_w.reshape(1, HEAD))


def _unit_lower_inverses(lows):
    r = lax.broadcasted_iota(jnp.int32, (CHUNK, CHUNK), 0)
    c = lax.broadcasted_iota(jnp.int32, (CHUNK, CHUNK), 1)
    eye = jnp.where(r == c, 1.0, 0.0).astype(F32)
    s = 1
    invs = None
    while s < CHUNK:
        sel = ((r // (2 * s)) == (c // (2 * s))) & ((r % (2 * s)) >= s) & ((c % (2 * s)) < s)
        low_s = [jnp.where(sel, low, 0.0) for low in lows]
        if s == 1:
            invs = [eye - x for x in low_s]
        else:
            inv_b = [_bf(x) for x in invs]
            tmp = [_bf(_dot(a, _bf(x))) for a, x in zip(inv_b, low_s)]
            invs = [x - _dot(t, a) for x, t, a in zip(invs, tmp, inv_b)]
        s *= 2
    return invs


def _gdn_kernel(q_ref, k_ref, v_ref, z_ref, gates_ref, wq_ref, wk_ref, wv_ref, avec_ref, dtb_ref,
                nw_ref, o_ref, xbuf, st_ref, *, tb):
    tt = pl.program_id(2)
    kh = pl.program_id(1)
    halo = SUBLANES
    n_ch = tb // CHUNK

    @pl.when(tt == 0)
    def _():
        st_ref[...] = jnp.zeros_like(st_ref)
        xbuf[0:halo, :] = jnp.zeros((halo, 4 * HEAD), F32)

    xbuf[halo:halo + tb, 0:HEAD] = q_ref[...].astype(F32)
    xbuf[halo:halo + tb, HEAD:2 * HEAD] = k_ref[...].astype(F32)
    xbuf[halo:halo + tb, 2 * HEAD:4 * HEAD] = v_ref[...].astype(F32)

    w_all = jnp.concatenate([wq_ref[...], wk_ref[...], wv_ref[...]], axis=1)
    ext = xbuf[...]
    conv = ext * w_all[0:1, :]
    for j in range(1, CONV_K):
        conv = ext * w_all[j:j + 1, :] + pltpu.roll(conv, 1, 0)
    conv = _silu(conv[halo:, :])
    xbuf[0:halo, :] = xbuf[tb:tb + halo, :]

    def l2n(x):
        return x * lax.rsqrt(jnp.sum(x * x, axis=-1, keepdims=True) + RMS_EPS)

    q_all = l2n(conv[:, 0:HEAD]) * (HEAD ** -0.5)
    k_all = l2n(conv[:, HEAD:2 * HEAD])

    gl = gates_ref[...]
    beta_all = jax.nn.sigmoid(gl)
    g_all = avec_ref[...] * _softplus(gl + dtb_ref[...])
    lane = lax.broadcasted_iota(jnp.int32, (tb, LANES), 1)

    r64 = lax.broadcasted_iota(jnp.int32, (CHUNK, CHUNK), 0)
    c64 = lax.broadcasted_iota(jnp.int32, (CHUNK, CHUNK), 1)
    tri = _bf(jnp.where(r64 >= c64, 1.0, 0.0))
    causal = r64 >= c64
    strict = r64 > c64
    strict_f = jnp.where(strict, 1.0, 0.0).astype(F32)

    chunk_rows = [slice(c * CHUNK, (c + 1) * CHUNK) for c in range(n_ch)]
    units = [(hv, c) for c in range(n_ch) for hv in range(2)]

    beta_cols, g_cols = [], []
    for hv in range(2):
        head = 2 * kh + hv
        beta_cols.append(jnp.sum(jnp.where(lane == head, beta_all, 0.0), axis=-1, keepdims=True))
        g_cols.append(jnp.sum(jnp.where(lane == N_V_HEADS + head, g_all, 0.0), axis=-1, keepdims=True))

    k_bf = [_bf(k_all[rows, :]) for rows in chunk_rows]
    kq = [_dot_nt(_bf(jnp.concatenate([k_all[rows, :], q_all[rows, :]], axis=0)), kb_)
          for rows, kb_ in zip(chunk_rows, k_bf)]

    decay, g_i, g_last = [], [], []
    for hv, c in units:
        gb = jnp.broadcast_to(g_cols[hv][chunk_rows[c], :], (CHUNK, CHUNK))
        wmat = jnp.concatenate([gb * strict_f, gb], axis=1)
        hi = _bf(wmat)
        lo = _bf(wmat - hi.astype(F32))
        cum = _dot(tri, hi) + _dot(tri, lo)
        decay.append(jnp.where(causal, jnp.exp(jnp.minimum(cum[:, :CHUNK], 0.0)), 0.0))
        g_i.append(cum[:, CHUNK:CHUNK + 1])
        g_last.append(cum[CHUNK - 1:CHUNK, CHUNK:CHUNK + 1])

    lows = [jnp.where(strict, kq[c][:CHUNK, :] * beta_cols[hv][chunk_rows[c], :] * decay[n], 0.0)
            for n, (hv, c) in enumerate(units)]
    invs = _unit_lower_inverses(lows)

    sols = []
    for n, (hv, c) in enumerate(units):
        rows = chunk_rows[c]
        beta = beta_cols[hv][rows, :]
        v = conv[rows, (2 + hv) * HEAD:(3 + hv) * HEAD]
        rhs = jnp.concatenate([v * beta, k_all[rows, :] * (beta * jnp.exp(g_i[n]))], axis=1)
        sols.append(_bf(_dot(_bf(invs[n]), _bf(rhs))))

    o_loc, q_eff, c_mat, p_mat = [], [], [], []
    for n, (hv, c) in enumerate(units):
        rows = chunk_rows[c]
        attn = _bf(kq[c][CHUNK:, :] * decay[n])
        auw = _dot(attn, sols[n])
        o_loc.append(auw[:, :HEAD])
        q_eff.append(_bf(q_all[rows, :] * jnp.exp(g_i[n]) - auw[:, HEAD:]))
        k_tail = _bf(k_all[rows, :] * jnp.exp(g_last[n] - g_i[n]))
        ktuw = _dot_tn(k_tail, sols[n])
        c_mat.append(ktuw[:, :HEAD])
        p_mat.append(_bf(ktuw[:, HEAD:]))

    states = [st_ref[0], st_ref[1]]
    outs = [None] * len(units)
    for n, (hv, c) in enumerate(units):
        sb = _bf(states[hv])
        outs[n] = _dot(q_eff[n], sb) + o_loc[n]
        states[hv] = states[hv] * jnp.exp(g_last[n]) + c_mat[n] - _dot(p_mat[n], sb)
    st_ref[0] = states[0]
    st_ref[1] = states[1]

    for n, (hv, c) in enumerate(units):
        rows = chunk_rows[c]
        o = outs[n]
        ms = jnp.mean(o * o, axis=-1, keepdims=True)
        o = o * lax.rsqrt(ms + RMS_EPS) * nw_ref[...]
        z = z_ref[rows, hv * HEAD:(hv + 1) * HEAD].astype(F32)
        o_ref[rows, hv * HEAD:(hv + 1) * HEAD] = _bf(o * _silu(z))


def _gdn(proj, gate_logits, conv_w, a_vec, dtb_vec, norm_w, tb):
    bsz, s, _ = proj.shape
    k_heads = D_MODEL // HEAD

    in_specs = [
        pl.BlockSpec((None, tb, HEAD), lambda b, h, t: (b, t, h)),
        pl.BlockSpec((None, tb, HEAD), lambda b, h, t: (b, t, k_heads + h)),
        pl.BlockSpec((None, tb, 2 * HEAD), lambda b, h, t: (b, t, k_heads + h)),
        pl.BlockSpec((None, tb, 2 * HEAD), lambda b, h, t: (b, t, 2 * k_heads + h)),
        pl.BlockSpec((None, tb, LANES), lambda b, h, t: (b, t, 0)),
        pl.BlockSpec((CONV_K, HEAD), lambda b, h, t: (0, h)),
        pl.BlockSpec((CONV_K, HEAD), lambda b, h, t: (0, k_heads + h)),
        pl.BlockSpec((CONV_K, 2 * HEAD), lambda b, h, t: (0, k_heads + h)),
        pl.BlockSpec((1, LANES), lambda b, h, t: (0, 0)),
        pl.BlockSpec((1, LANES), lambda b, h, t: (0, 0)),
        pl.BlockSpec((1, HEAD), lambda b, h, t: (0, 0)),
    ]
    return pl.pallas_call(
        functools.partial(_gdn_kernel, tb=tb),
        grid=(bsz, k_heads, s // tb),
        in_specs=in_specs,
        out_specs=pl.BlockSpec((None, tb, 2 * HEAD), lambda b, h, t: (b, t, h)),
        out_shape=jax.ShapeDtypeStruct((bsz, s, 2 * D_MODEL), BF16),
        scratch_shapes=[pltpu.VMEM((tb + SUBLANES, 4 * HEAD), F32),
                        pltpu.VMEM((2, HEAD, HEAD), F32)],
        compiler_params=_params("parallel", "parallel", "arbitrary"),
        name="gated_deltanet",
    )(proj, proj, proj, proj, gate_logits, conv_w, conv_w, conv_w, a_vec, dtb_vec,
      norm_w.reshape(1, HEAD))


def kernel(x, norm_w, final_norm_w, even_w_in, even_w_out, hgrn_lb_logits, hgrn_norm_w,
           odd_w_in, odd_conv_w, odd_dt_bias, odd_a_log, odd_norm_w, odd_w_out):
    bsz, s, d = x.shape
    depth = norm_w.shape[0]
    t = bsz * s
    tm = 512
    tm_in = 1024
    tb = 512

    lb_all = jnp.cumsum(jax.nn.softmax(hgrn_lb_logits.astype(F32), axis=0), axis=0)
    lb_all = jnp.maximum(lb_all - lb_all[0:1], 0.0)
    log_lb = jnp.log(lb_all)
    log_1mlb = jnp.log1p(-lb_all)

    h = x.reshape(t, d)
    for layer in range(depth):
        j = layer // 2
        final_w = final_norm_w if layer == depth - 1 else None
        if layer % 2 == 0:
            proj, f_logit = _norm_matmul(h, norm_w[layer], _bf(even_w_in[j]), tm_in, d,
                                         side_start=5 * d, side_width=d)
            proj = proj.reshape(bsz, s, EVEN_IN)
            a_mix = _dilated_attention(proj)
            b_mix = _hgrn(proj, f_logit.reshape(bsz, s, d), log_lb[j], log_1mlb[j], hgrn_norm_w[j], tb)
            w_out = _bf(even_w_out[j])
            h = _out_proj(h, [a_mix.reshape(t, d), b_mix.reshape(t, d)], [w_out[:d], w_out[d:]],
                          final_w, tm)
        else:
            pad = jnp.zeros((d, ODD_IN_PAD - odd_w_in.shape[2]), F32)
            w_in = _bf(jnp.concatenate([odd_w_in[j], pad], axis=1))
            proj, gate_logits = _norm_matmul(h, norm_w[layer], w_in, tm_in, ODD_IN_PAD // 7,
                                             side_start=ODD_MAIN, side_width=LANES)
            proj = proj.reshape(bsz, s, ODD_IN_PAD)
            zeros16 = jnp.zeros((N_V_HEADS,), F32)
            tail = jnp.zeros((LANES - 2 * N_V_HEADS,), F32)
            a_vec = jnp.concatenate([zeros16, -jnp.exp(odd_a_log[j].astype(F32)), tail]).reshape(1, LANES)
            dtb_vec = jnp.concatenate([zeros16, odd_dt_bias[j].astype(F32), tail]).reshape(1, LANES)
            o = _gdn(proj, gate_logits.reshape(bsz, s, LANES), odd_conv_w[j], a_vec, dtb_vec,
                     odd_norm_w[j], tb)
            h = _out_proj(h, [o.reshape(t, 2 * d)], [_bf(odd_w_out[j])], final_w, tm)
    return h.reshape(bsz, s, d)
```

```python
import functools

import jax
import jax.numpy as jnp
from jax import lax
from jax.experimental import pallas as pl
from jax.experimental.pallas import tpu as pltpu

F32 = jnp.float32
BF16 = jnp.bfloat16
HIGHEST = lax.Precision.HIGHEST

RMS_EPS = 1e-6
LANES = 128
SUBLANES = 8
VMEM_LIMIT = 48 * 1024 * 1024

D_MODEL = 1024
A_HEAD_DIM = 64
A_BLOCK = 128
A_DILATIONS = (1, 4, 16)
CHUNK = 64
A_SCALE = A_HEAD_DIM ** -0.5 * 1.4426950408889634
A_MOD = 16
A_GROUP = 8
HEAD = 128
CONV_K = 4
GDN_KH = 2

EVEN_IN = 8 * D_MODEL
ODD_CONV = 4 * D_MODEL
QK_TILES = 2
ODD_REST_PAD = 2 * D_MODEL + LANES
N_V_HEADS = 16


def _dot(a, b):
    return jnp.dot(a, b, preferred_element_type=F32)


def _dot_nt(a, b, precision=None):
    return lax.dot_general(a, b, (((1,), (1,)), ((), ())), precision=precision,
                           preferred_element_type=F32)


def _dot_tn(a, b):
    return lax.dot_general(a, b, (((0,), (0,)), ((), ())), preferred_element_type=F32)


def _bf(x):
    return x.astype(BF16)


def _silu(x):
    return x * jax.nn.sigmoid(x)


def _softplus(x):
    return jnp.maximum(x, 0.0) + jnp.log1p(jnp.exp(-jnp.abs(x)))


def _params(*sem):
    return pltpu.CompilerParams(dimension_semantics=sem, vmem_limit_bytes=VMEM_LIMIT)


def _norm_matmul_kernel(x_ref, nw_ref, w_ref, *rest, side_tile, side_lo, forget_gate):
    if forget_gate:
        loglb_ref, log1mlb_ref, o_ref, side_ref, xn_ref = rest
    else:
        o_ref, side_ref, xn_ref = rest
    j = pl.program_id(1)

    @pl.when(j == 0)
    def _():
        x = x_ref[...]
        ms = jnp.mean(x * x, axis=-1, keepdims=True)
        xn_ref[...] = _bf(x * lax.rsqrt(ms + RMS_EPS) * nw_ref[...])

    acc = _dot(xn_ref[...], w_ref[...])

    if not forget_gate:
        o_ref[...] = _bf(acc)

        @pl.when(j == side_tile)
        def _():
            side_ref[...] = acc[:, side_lo:side_lo + side_ref.shape[1]]
    else:
        @pl.when(j != side_tile)
        def _():
            o_ref[...] = _bf(acc)

        @pl.when(j == side_tile)
        def _():
            log_sig = jnp.minimum(acc, 0.0) - jnp.log1p(jnp.exp(-jnp.abs(acc)))
            log_lb = loglb_ref[...]
            b = log1mlb_ref[...] + log_sig
            side_ref[...] = jnp.maximum(log_lb, b) + jnp.log1p(jnp.exp(-jnp.abs(log_lb - b)))
            o_ref[...] = _bf(jnp.exp(b - acc))


def _norm_matmul(x, nw, w, tm, tn, side_start, side_width, forget_params=None):
    t, d = x.shape
    n = w.shape[1]
    side_tile, side_lo = divmod(side_start, tn)
    assert side_lo + side_width <= tn
    forget_gate = forget_params is not None
    extra_specs, extra_args = [], []
    if forget_gate:
        assert side_lo == 0 and side_width == tn
        extra_specs = [pl.BlockSpec((1, tn), lambda i, j: (0, 0))] * 2
        extra_args = [p.reshape(1, tn) for p in forget_params]
    return pl.pallas_call(
        functools.partial(_norm_matmul_kernel, side_tile=side_tile, side_lo=side_lo,
                          forget_gate=forget_gate),
        grid=(t // tm, n // tn),
        in_specs=[pl.BlockSpec((tm, d), lambda i, j: (i, 0)),
                  pl.BlockSpec((1, d), lambda i, j: (0, 0)),
                  pl.BlockSpec((d, tn), lambda i, j: (0, j))] + extra_specs,
        out_specs=[pl.BlockSpec((tm, tn), lambda i, j: (i, j)),
                   pl.BlockSpec((tm, side_width), lambda i, j: (i, 0))],
        out_shape=[jax.ShapeDtypeStruct((t, n), BF16),
                   jax.ShapeDtypeStruct((t, side_width), F32)],
        scratch_shapes=[pltpu.VMEM((tm, d), BF16)],
        compiler_params=_params("parallel", "arbitrary"),
        name="norm_in_proj",
    )(x, nw.reshape(1, d), w, *extra_args)


def _qkv_conv_kernel(x_ref, nw_ref, w_ref, cw_ref, o_ref, xn_ref, halo_ref, *, tiles_per_seq):
    i = pl.program_id(0)
    j = pl.program_id(1)
    tm, tn = o_ref.shape
    halo = halo_ref.shape[1]

    @pl.when(j == 0)
    def _():
        x = x_ref[...]
        ms = jnp.mean(x * x, axis=-1, keepdims=True)
        xn_ref[...] = _bf(x * lax.rsqrt(ms + RMS_EPS) * nw_ref[...])

    @pl.when(i == 0)
    def _():
        halo_ref[j] = jnp.zeros((halo, tn), F32)

    acc = _dot(xn_ref[...], w_ref[...])
    prev = jnp.where(i % tiles_per_seq == 0, 0.0, halo_ref[j])
    halo_ref[j] = acc[tm - halo:, :]
    ext = jnp.concatenate([prev, acc], axis=0)
    cw = cw_ref[...]
    conv = ext * cw[0:1, :]
    for k in range(1, CONV_K):
        conv = ext * cw[k:k + 1, :] + pltpu.roll(conv, 1, 0)
    y = _silu(conv[halo:, :])

    @pl.when(j < QK_TILES)
    def _():
        scale = jnp.where(j == 0, HEAD ** -0.5, 1.0)
        for h0 in range(0, tn, HEAD):
            yh = y[:, h0:h0 + HEAD]
            inv = lax.rsqrt(jnp.sum(yh * yh, axis=-1, keepdims=True) + RMS_EPS) * scale
            o_ref[:, h0:h0 + HEAD] = _bf(yh * inv)

    @pl.when(j >= QK_TILES)
    def _():
        o_ref[...] = _bf(y)


def _qkv_conv_proj(x, nw, w, conv_w, tm, seq):
    t, d = x.shape
    n = w.shape[1]
    tn = d
    return pl.pallas_call(
        functools.partial(_qkv_conv_kernel, tiles_per_seq=seq // tm),
        grid=(t // tm, n // tn),
        in_specs=[pl.BlockSpec((tm, d), lambda i, j: (i, 0)),
                  pl.BlockSpec((1, d), lambda i, j: (0, 0)),
                  pl.BlockSpec((d, tn), lambda i, j: (0, j)),
                  pl.BlockSpec((CONV_K, tn), lambda i, j: (0, j))],
        out_specs=pl.BlockSpec((tm, tn), lambda i, j: (i, j)),
        out_shape=jax.ShapeDtypeStruct((t, n), BF16),
        scratch_shapes=[pltpu.VMEM((tm, d), BF16),
                        pltpu.VMEM((n // tn, SUBLANES, tn), F32)],
        compiler_params=_params("arbitrary", "arbitrary"),
        name="qkv_conv_proj",
    )(x, nw.reshape(1, d), w, conv_w)


def _out_proj_kernel(*refs, n_in, final):
    h_ref = refs[0]
    a_refs = refs[1:1 + n_in]
    w_refs = refs[1 + n_in:1 + 2 * n_in]
    o_ref = refs[-1]
    acc = h_ref[...]
    for a_ref, w_ref in zip(a_refs, w_refs):
        acc = acc + _dot(a_ref[...], w_ref[...])
    if final:
        fw_ref = refs[1 + 2 * n_in]
        ms = jnp.mean(acc * acc, axis=-1, keepdims=True)
        acc = acc * lax.rsqrt(ms + RMS_EPS) * fw_ref[...]
    o_ref[...] = acc


def _out_proj(h, acts, ws, final_w, tm):
    t, d = h.shape
    n_in = len(acts)
    final = final_w is not None
    in_specs = [pl.BlockSpec((tm, d), lambda i: (i, 0))]
    in_specs += [pl.BlockSpec((tm, a.shape[1]), lambda i: (i, 0)) for a in acts]
    in_specs += [pl.BlockSpec(w.shape, lambda i: (0, 0)) for w in ws]
    args = [h, *acts, *ws]
    if final:
        in_specs.append(pl.BlockSpec((1, d), lambda i: (0, 0)))
        args.append(final_w.reshape(1, d))
    return pl.pallas_call(
        functools.partial(_out_proj_kernel, n_in=n_in, final=final),
        grid=(t // tm,),
        in_specs=in_specs,
        out_specs=pl.BlockSpec((tm, d), lambda i: (i, 0)),
        out_shape=jax.ShapeDtypeStruct((t, d), F32),
        compiler_params=_params("parallel"),
        name="out_proj",
    )(*args)


def _attn_kernel(q_ref, k_ref, v_ref, gate_ref, o_ref, qp_ref, kp_ref, vp_ref, acc_ref, m_ref, l_ref,
                 bias_ref, *, seq):
    per_res = seq // A_MOD
    n_blocks = seq // A_BLOCK
    row = lax.broadcasted_iota(jnp.int32, (A_BLOCK, 2 * A_BLOCK), 0)
    col = lax.broadcasted_iota(jnp.int32, (A_BLOCK, 2 * A_BLOCK), 1)
    lane = lax.broadcasted_iota(jnp.int32, (A_BLOCK, LANES), 1)
    head0 = lane < A_HEAD_DIM

    def widen(t, carry):
        rows = pl.ds(pl.multiple_of(t * A_BLOCK, A_BLOCK), A_BLOCK)
        acc_ref[rows, :] = q_ref[rows, :].astype(F32) * A_SCALE
        m_ref[rows, :] = k_ref[rows, :].astype(F32)
        l_ref[rows, :] = v_ref[rows, :].astype(F32)
        return carry

    lax.fori_loop(0, n_blocks, widen, 0)

    def to_residue_major(res, carry):
        src = pl.ds(res, per_res, stride=A_MOD)
        dst = pl.ds(pl.multiple_of(res * per_res, per_res), per_res)
        qp_ref[dst, :] = acc_ref[src, :]
        kp_ref[dst, :] = m_ref[src, :]
        vp_ref[dst, :] = l_ref[src, :]
        return carry

    lax.fori_loop(0, A_MOD, to_residue_major, 0)

    for dilation in A_DILATIONS:
        first = dilation == A_DILATIONS[0]
        per_class = n_blocks // dilation
        runs = A_MOD // dilation
        rq = A_BLOCK // runs
        dist = runs * (row % rq - col % (2 * rq)) + (row // rq - col // (2 * rq)) + A_BLOCK
        band = (dist >= 0) & (dist <= A_BLOCK)
        in_cur = col % (2 * rq) >= rq
        slot = 2 * A_DILATIONS.index(dilation)
        bias_ref[slot] = jnp.where(band & in_cur, 0.0, -jnp.inf)
        bias_ref[slot + 1] = jnp.where(band, 0.0, -jnp.inf)

        def load(n, dilation=dilation, first=first, per_class=per_class, runs=runs, rq=rq):
            res = n // per_class
            i = n % per_class
            cur, prev = [], []
            for c in range(runs):
                base = pl.multiple_of((c * dilation + res) * per_res + rq * i, SUBLANES)
                cur.append(pl.ds(base, rq))
                prev.append(pl.ds(pl.multiple_of(jnp.where(i > 0, base - rq, base), SUBLANES), rq))

            def window(ref):
                return _bf(jnp.concatenate([ref[rows, :] for pc in zip(prev, cur) for rows in pc], axis=0))

            def block(ref):
                return jnp.concatenate([ref[rows, :] for rows in cur], axis=0)

            return cur, i, block(qp_ref), window(kp_ref), window(vp_ref)

        def scores(i, q, kw, vw, slot=slot):
            bias = bias_ref[slot + (i > 0).astype(jnp.int32)]
            parts = []
            for h in range(2):
                qh = _bf(jnp.where(head0 if h == 0 else ~head0, q, 0.0))
                s = _dot_nt(qh, kw) + bias
                m_blk = jnp.max(s, axis=-1, keepdims=True)
                p = jnp.exp2(s - m_blk)
                parts.append((_dot(_bf(p), vw), m_blk, jnp.sum(p, axis=-1, keepdims=True)))
            return tuple(jnp.where(head0, a, b) for a, b in zip(*parts))

        def group(t, carry, load=load, scores=scores, first=first, rq=rq):
            loaded = [load(t * A_GROUP + u) for u in range(A_GROUP)]
            results = [scores(*item[1:]) for item in loaded]
            for (cur, *_), (acc, m_blk, l_blk) in zip(loaded, results):
                if not first:
                    acc_old, m_old, l_old = (
                        jnp.concatenate([ref[rows, :] for rows in cur], axis=0)
                        for ref in (acc_ref, m_ref, l_ref))
                    m_new = jnp.maximum(m_old, m_blk)
                    alpha = jnp.exp2(m_old - m_new)
                    beta = jnp.exp2(m_blk - m_new)
                    acc = acc_old * alpha + acc * beta
                    l_blk = l_old * alpha + l_blk * beta
                    m_blk = m_new
                for ref, val in zip((acc_ref, m_ref, l_ref), (acc, m_blk, l_blk)):
                    for c, rows in enumerate(cur):
                        ref[rows, :] = val[c * rq:(c + 1) * rq, :]
            return carry

        lax.fori_loop(0, n_blocks // A_GROUP, group, 0)

    def to_position_order(res, carry):
        src = pl.ds(pl.multiple_of(res * per_res, per_res), per_res)
        m_ref[pl.ds(res, per_res, stride=A_MOD), :] = acc_ref[src, :] / l_ref[src, :]
        return carry

    lax.fori_loop(0, A_MOD, to_position_order, 0)

    def finish(t, carry):
        rows = pl.ds(pl.multiple_of(t * A_BLOCK, A_BLOCK), A_BLOCK)
        o_ref[rows, :] = _bf(m_ref[rows, :] * _silu(gate_ref[rows, :].astype(F32)))
        return carry

    lax.fori_loop(0, n_blocks, finish, 0)


def _dilated_attention(proj):
    bsz, s, _ = proj.shape
    hp_blocks = D_MODEL // LANES

    def col_spec(off):
        return pl.BlockSpec((None, s, LANES), lambda b, hp: (b, 0, off * hp_blocks + hp))

    return pl.pallas_call(
        functools.partial(_attn_kernel, seq=s),
        grid=(bsz, hp_blocks),
        in_specs=[col_spec(0), col_spec(1), col_spec(2), col_spec(3)],
        out_specs=pl.BlockSpec((None, s, LANES), lambda b, hp: (b, 0, hp)),
        out_shape=jax.ShapeDtypeStruct((bsz, s, D_MODEL), BF16),
        scratch_shapes=[pltpu.VMEM((s, LANES), F32)] * 6
        + [pltpu.VMEM((2 * len(A_DILATIONS), A_BLOCK, 2 * A_BLOCK), F32)],
        compiler_params=_params("parallel", "parallel"),
        name="dilated_attn",
    )(proj, proj, proj, proj)


def _hgrn_kernel(q_ref, logf_ref, k_ref, i_ref, gate_ref, nw_ref, o_ref, st_ref, *, tb):
    @pl.when(pl.program_id(2) == 0)
    def _():
        st_ref[...] = jnp.zeros_like(st_ref)

    n_ch = tb // CHUNK
    r64 = lax.broadcasted_iota(jnp.int32, (CHUNK, CHUNK), 0)
    c64 = lax.broadcasted_iota(jnp.int32, (CHUNK, CHUNK), 1)
    tri = _bf(jnp.where(r64 >= c64, 1.0, 0.0))
    row = lax.broadcasted_iota(jnp.int32, (CHUNK, HEAD), 0)
    sub = lax.broadcasted_iota(jnp.int32, (SUBLANES, HEAD), 0)
    log_f_all = logf_ref[...]

    chunk_rows = [slice(c * CHUNK, (c + 1) * CHUNK) for c in range(n_ch)]
    qs = [q_ref[rows, :].astype(F32) for rows in chunk_rows]
    ks = [k_ref[rows, :].astype(F32) for rows in chunk_rows]
    vbs = [i_ref[rows, :] for rows in chunk_rows]

    gs = []
    for rows in chunk_rows:
        lf = log_f_all[rows, :]
        hi = _bf(lf)
        r1 = lf - hi.astype(F32)
        mid = _bf(r1)
        lo = _bf(r1 - mid.astype(F32))
        cum = _dot(tri, jnp.concatenate([hi, mid, lo], axis=1))
        gs.append(cum[:, :HEAD] + cum[:, HEAD:2 * HEAD] + cum[:, 2 * HEAD:])

    attn = [None] * n_ch
    s = CHUNK // 2
    while s >= 1:
        blk = 2 * s
        upper = (row % blk) >= s
        mask = ((r64 // blk) == (c64 // blk)) & ((r64 % blk) >= s) & ((c64 % blk) < s)
        for c in range(n_ch):
            g = gs[c]
            if blk >= SUBLANES:
                g_ref = jnp.concatenate(
                    [jnp.broadcast_to(g[b0 + s - 1:b0 + s, :], (blk, HEAD)) for b0 in range(0, CHUNK, blk)],
                    axis=0)
            elif s == 2:
                g_ref = jnp.concatenate(
                    [jnp.where(sub < 4, jnp.broadcast_to(g[v0 + 1:v0 + 2, :], (SUBLANES, HEAD)),
                               jnp.broadcast_to(g[v0 + 5:v0 + 6, :], (SUBLANES, HEAD)))
                     for v0 in range(0, CHUNK, SUBLANES)], axis=0)
            else:
                g_ref = jnp.where(upper, pltpu.roll(g, 1, 0), g)
            z = _bf(jnp.where(upper, qs[c], ks[c]) * jnp.exp(-jnp.abs(g - g_ref)))
            m = _dot_nt(z, z)
            attn[c] = jnp.where(mask, m, 0.0) if attn[c] is None else jnp.where(mask, m, attn[c])
        s //= 2
    eye = r64 == c64
    attn = [jnp.where(eye, jnp.sum(qs[c] * ks[c], axis=-1, keepdims=True), attn[c]) for c in range(n_ch)]

    o_intra = [_dot(_bf(attn[c]), vbs[c]) for c in range(n_ch)]
    q_dec = [_bf(qs[c] * jnp.exp(gs[c])) for c in range(n_ch)]
    g_last = [gs[c][CHUNK - 1:CHUNK, :] for c in range(n_ch)]
    kv = [_dot_tn(vbs[c], _bf(ks[c] * jnp.exp(g_last[c] - gs[c]))) for c in range(n_ch)]

    state = st_ref[...]
    outs = []
    for c in range(n_ch):
        outs.append(_dot_nt(q_dec[c], _bf(state)) + o_intra[c])
        state = state * jnp.exp(g_last[c]) + kv[c]
    st_ref[...] = state

    for c, rows in enumerate(chunk_rows):
        o = outs[c]
        ms = jnp.mean(o * o, axis=-1, keepdims=True)
        o = o * lax.rsqrt(ms + RMS_EPS) * nw_ref[...]
        o_ref[rows, :] = _bf(o * _silu(gate_ref[rows, :].astype(F32)))


def _hgrn(proj, log_f, norm_w, tb):
    bsz, s, _ = proj.shape
    heads = D_MODEL // HEAD
    base = 4 * heads

    def col_spec(off):
        return pl.BlockSpec((None, tb, HEAD), lambda b, h, t: (b, t, base + off * heads + h))

    return pl.pallas_call(
        functools.partial(_hgrn_kernel, tb=tb),
        grid=(bsz, heads, s // tb),
        in_specs=[col_spec(0), pl.BlockSpec((None, tb, HEAD), lambda b, h, t: (b, t, h)),
                  col_spec(1), col_spec(2), col_spec(3),
                  pl.BlockSpec((1, HEAD), lambda b, h, t: (0, 0))],
        out_specs=pl.BlockSpec((None, tb, HEAD), lambda b, h, t: (b, t, h)),
        out_shape=jax.ShapeDtypeStruct((bsz, s, D_MODEL), BF16),
        scratch_shapes=[pltpu.VMEM((HEAD, HEAD), F32)],
        compiler_params=_params("parallel", "parallel", "arbitrary"),
        name="hgrn2",
    )(proj, log_f, proj, proj, proj, norm_w.reshape(1, HEAD))


def _unit_lower_inverses(lows):
    r = lax.broadcasted_iota(jnp.int32, (CHUNK, CHUNK), 0)
    c = lax.broadcasted_iota(jnp.int32, (CHUNK, CHUNK), 1)
    eye = jnp.where(r == c, 1.0, 0.0).astype(F32)
    s = 1
    invs = None
    while s < CHUNK:
        sel = ((r // (2 * s)) == (c // (2 * s))) & ((r % (2 * s)) >= s) & ((c % (2 * s)) < s)
        low_s = [jnp.where(sel, low, 0.0) for low in lows]
        if s == 1:
            invs = [eye - x for x in low_s]
        else:
            inv_b = [_bf(x) for x in invs]
            tmp = [_bf(_dot(a, _bf(x))) for a, x in zip(inv_b, low_s)]
            invs = [x - _dot(t, a) for x, t, a in zip(invs, tmp, inv_b)]
        s *= 2
    return invs


def _gdn_kernel(q_ref, k_ref, v_ref, z_ref, gates_ref, avec_ref, dtb_ref, nw_ref, o_ref, st_ref, *, tb):
    tt = pl.program_id(2)
    group = pl.program_id(1)
    n_ch = tb // CHUNK

    @pl.when(tt == 0)
    def _():
        st_ref[...] = jnp.zeros_like(st_ref)

    q_all = [q_ref[:, kk * HEAD:(kk + 1) * HEAD].astype(F32) for kk in range(GDN_KH)]
    k_all = [k_ref[:, kk * HEAD:(kk + 1) * HEAD].astype(F32) for kk in range(GDN_KH)]

    gl = gates_ref[...]
    beta_all = jax.nn.sigmoid(gl)
    g_all = avec_ref[...] * _softplus(gl + dtb_ref[...])
    lane = lax.broadcasted_iota(jnp.int32, (tb, LANES), 1)

    r64 = lax.broadcasted_iota(jnp.int32, (CHUNK, CHUNK), 0)
    c64 = lax.broadcasted_iota(jnp.int32, (CHUNK, CHUNK), 1)
    tri = _bf(jnp.where(r64 >= c64, 1.0, 0.0))
    causal = r64 >= c64
    strict = r64 > c64
    strict_f = jnp.where(strict, 1.0, 0.0).astype(F32)

    chunk_rows = [slice(c * CHUNK, (c + 1) * CHUNK) for c in range(n_ch)]
    n_vh = 2 * GDN_KH
    units = [(vh, c) for c in range(n_ch) for vh in range(n_vh)]

    beta_cols, g_cols = [], []
    for vh in range(n_vh):
        head = n_vh * group + vh
        beta_cols.append(jnp.sum(jnp.where(lane == head, beta_all, 0.0), axis=-1, keepdims=True))
        g_cols.append(jnp.sum(jnp.where(lane == N_V_HEADS + head, g_all, 0.0), axis=-1, keepdims=True))

    kq = [[_dot_nt(_bf(jnp.concatenate([k_all[kk][rows, :], q_all[kk][rows, :]], axis=0)),
                   _bf(k_all[kk][rows, :])) for rows in chunk_rows]
          for kk in range(GDN_KH)]

    decay, g_i, g_last = [], [], []
    for vh, c in units:
        gb = jnp.broadcast_to(g_cols[vh][chunk_rows[c], :], (CHUNK, CHUNK))
        wmat = jnp.concatenate([gb * strict_f, gb], axis=1)
        hi = _bf(wmat)
        lo = _bf(wmat - hi.astype(F32))
        cum = _dot(tri, hi) + _dot(tri, lo)
        decay.append(jnp.where(causal, jnp.exp(jnp.minimum(cum[:, :CHUNK], 0.0)), 0.0))
        g_i.append(cum[:, CHUNK:CHUNK + 1])
        g_last.append(cum[CHUNK - 1:CHUNK, CHUNK:CHUNK + 1])

    lows = [jnp.where(strict, kq[vh // 2][c][:CHUNK, :] * beta_cols[vh][chunk_rows[c], :] * decay[n], 0.0)
            for n, (vh, c) in enumerate(units)]
    invs = _unit_lower_inverses(lows)

    sols = []
    for n, (vh, c) in enumerate(units):
        rows = chunk_rows[c]
        beta = beta_cols[vh][rows, :]
        v = v_ref[rows, vh * HEAD:(vh + 1) * HEAD].astype(F32)
        rhs = jnp.concatenate([v * beta, k_all[vh // 2][rows, :] * (beta * jnp.exp(g_i[n]))], axis=1)
        sols.append(_bf(_dot(_bf(invs[n]), _bf(rhs))))

    o_loc, q_eff, c_mat, p_mat = [], [], [], []
    for n, (vh, c) in enumerate(units):
        rows = chunk_rows[c]
        attn = _bf(kq[vh // 2][c][CHUNK:, :] * decay[n])
        auw = _dot(attn, sols[n])
        o_loc.append(auw[:, :HEAD])
        q_eff.append(_bf(q_all[vh // 2][rows, :] * jnp.exp(g_i[n]) - auw[:, HEAD:]))
        k_tail = _bf(k_all[vh // 2][rows, :] * jnp.exp(g_last[n] - g_i[n]))
        ktuw = _dot_tn(k_tail, sols[n])
        c_mat.append(ktuw[:, :HEAD])
        p_mat.append(_bf(ktuw[:, HEAD:]))

    states = [st_ref[vh] for vh in range(n_vh)]
    outs = [None] * len(units)
    for n, (vh, c) in enumerate(units):
        sb = _bf(states[vh])
        outs[n] = _dot(q_eff[n], sb) + o_loc[n]
        states[vh] = states[vh] * jnp.exp(g_last[n]) + c_mat[n] - _dot(p_mat[n], sb)
    for vh in range(n_vh):
        st_ref[vh] = states[vh]

    for n, (vh, c) in enumerate(units):
        rows = chunk_rows[c]
        o = outs[n]
        ms = jnp.mean(o * o, axis=-1, keepdims=True)
        o = o * lax.rsqrt(ms + RMS_EPS) * nw_ref[...]
        z = z_ref[rows, vh * HEAD:(vh + 1) * HEAD].astype(F32)
        o_ref[rows, vh * HEAD:(vh + 1) * HEAD] = _bf(o * _silu(z))


def _gdn(qkv, z, gate_logits, a_vec, dtb_vec, norm_w, tb):
    bsz, s, _ = qkv.shape
    width = GDN_KH * HEAD
    groups = D_MODEL // width

    in_specs = [
        pl.BlockSpec((None, tb, width), lambda b, h, t: (b, t, h)),
        pl.BlockSpec((None, tb, width), lambda b, h, t: (b, t, groups + h)),
        pl.BlockSpec((None, tb, 2 * width), lambda b, h, t: (b, t, groups + h)),
        pl.BlockSpec((None, tb, 2 * width), lambda b, h, t: (b, t, h)),
        pl.BlockSpec((None, tb, LANES), lambda b, h, t: (b, t, 0)),
        pl.BlockSpec((1, LANES), lambda b, h, t: (0, 0)),
        pl.BlockSpec((1, LANES), lambda b, h, t: (0, 0)),
        pl.BlockSpec((1, HEAD), lambda b, h, t: (0, 0)),
    ]
    return pl.pallas_call(
        functools.partial(_gdn_kernel, tb=tb),
        grid=(bsz, groups, s // tb),
        in_specs=in_specs,
        out_specs=pl.BlockSpec((None, tb, 2 * width), lambda b, h, t: (b, t, h)),
        out_shape=jax.ShapeDtypeStruct((bsz, s, 2 * D_MODEL), BF16),
        scratch_shapes=[pltpu.VMEM((2 * GDN_KH, HEAD, HEAD), F32)],
        compiler_params=_params("parallel", "parallel", "arbitrary"),
        name="gated_deltanet",
    )(qkv, qkv, qkv, z, gate_logits, a_vec, dtb_vec, norm_w.reshape(1, HEAD))


def kernel(x, norm_w, final_norm_w, even_w_in, even_w_out, hgrn_lb_logits, hgrn_norm_w,
           odd_w_in, odd_conv_w, odd_dt_bias, odd_a_log, odd_norm_w, odd_w_out):
    bsz, s, d = x.shape
    depth = norm_w.shape[0]
    t = bsz * s
    tm = 512
    tm_in = 1024
    tb = 512

    lb_all = jnp.cumsum(jax.nn.softmax(hgrn_lb_logits.astype(F32), axis=0), axis=0)
    lb_all = jnp.maximum(lb_all - lb_all[0:1], 0.0)
    log_lb = jnp.log(lb_all)
    log_1mlb = jnp.log1p(-lb_all)

    h = x.reshape(t, d)
    for layer in range(depth):
        j = layer // 2
        final_w = final_norm_w if layer == depth - 1 else None
        if layer % 2 == 0:
            proj, log_f = _norm_matmul(h, norm_w[layer], _bf(even_w_in[j]), tm_in, d,
                                       side_start=5 * d, side_width=d,
                                       forget_params=(log_lb[j], log_1mlb[j]))
            proj = proj.reshape(bsz, s, EVEN_IN)
            a_mix = _dilated_attention(proj)
            b_mix = _hgrn(proj, log_f.reshape(bsz, s, d), hgrn_norm_w[j], tb)
            w_out = _bf(even_w_out[j])
            h = _out_proj(h, [a_mix.reshape(t, d), b_mix.reshape(t, d)], [w_out[:d], w_out[d:]],
                          final_w, tm)
        else:
            n_rest = odd_w_in.shape[2] - ODD_CONV
            pad = jnp.zeros((d, ODD_REST_PAD - n_rest), F32)
            w_qkv = _bf(odd_w_in[j][:, :ODD_CONV])
            w_rest = _bf(jnp.concatenate([odd_w_in[j][:, ODD_CONV:], pad], axis=1))
            qkv = _qkv_conv_proj(h, norm_w[layer], w_qkv, odd_conv_w[j], tm_in, s)
            z, gate_logits = _norm_matmul(h, norm_w[layer], w_rest, tm_in, ODD_REST_PAD,
                                          side_start=ODD_REST_PAD - LANES, side_width=LANES)
            zeros16 = jnp.zeros((N_V_HEADS,), F32)
            tail = jnp.zeros((LANES - 2 * N_V_HEADS,), F32)
            a_vec = jnp.concatenate([zeros16, -jnp.exp(odd_a_log[j].astype(F32)), tail]).reshape(1, LANES)
            dtb_vec = jnp.concatenate([zeros16, odd_dt_bias[j].astype(F32), tail]).reshape(1, LANES)
            o = _gdn(qkv.reshape(bsz, s, ODD_CONV), z.reshape(bsz, s, ODD_REST_PAD),
                     gate_logits.reshape(bsz, s, LANES), a_vec, dtb_vec, odd_norm_w[j], tb)
            h = _out_proj(h, [o.reshape(t, 2 * d)], [_bf(odd_w_out[j])], final_w, tm)
    return h.reshape(bsz, s, d)
```

```python
import functools

import jax
import jax.numpy as jnp
from jax import lax
from jax.experimental import pallas as pl
from jax.experimental.pallas import tpu as pltpu

F32 = jnp.float32
BF16 = jnp.bfloat16
HIGHEST = lax.Precision.HIGHEST

RMS_EPS = 1e-6
LANES = 128
SUBLANES = 8
VMEM_LIMIT = 48 * 1024 * 1024

D_MODEL = 1024
A_HEAD_DIM = 64
A_BLOCK = 128
A_DILATIONS = (1, 4, 16)
CHUNK = 64
A_SCALE = A_HEAD_DIM ** -0.5 * 1.4426950408889634
A_MOD = 16
A_GROUP = 8
HEAD = 128
CONV_K = 4
HGRN_KH = 2
GDN_KH = 2

EVEN_IN = 8 * D_MODEL
ODD_CONV = 4 * D_MODEL
QK_TILES = 2
ODD_REST_PAD = 2 * D_MODEL + LANES
N_V_HEADS = 16


def _dot(a, b):
    return jnp.dot(a, b, preferred_element_type=F32)


def _dot_nt(a, b, precision=None):
    return lax.dot_general(a, b, (((1,), (1,)), ((), ())), precision=precision,
                           preferred_element_type=F32)


def _dot_tn(a, b):
    return lax.dot_general(a, b, (((0,), (0,)), ((), ())), preferred_element_type=F32)


def _bf(x):
    return x.astype(BF16)


def _silu(x):
    return x * jax.nn.sigmoid(x)


def _softplus(x):
    return jnp.maximum(x, 0.0) + jnp.log1p(jnp.exp(-jnp.abs(x)))


def _params(*sem):
    return pltpu.CompilerParams(dimension_semantics=sem, vmem_limit_bytes=VMEM_LIMIT)


def _norm_rows(x_ref, nw_ref, xn_ref):
    x = x_ref[...]
    ms = jnp.mean(x * x, axis=-1, keepdims=True)
    xn_ref[...] = _bf(x * lax.rsqrt(ms + RMS_EPS) * nw_ref[...])


def _norm_matmul_kernel(x_ref, nw_ref, w_ref, o_ref, side_ref, xn_ref, *, side_tile, side_lo):
    j = pl.program_id(1)
    pl.when(j == 0)(functools.partial(_norm_rows, x_ref, nw_ref, xn_ref))
    acc = _dot(xn_ref[...], w_ref[...])
    o_ref[...] = _bf(acc)

    @pl.when(j == side_tile)
    def _():
        side_ref[...] = acc[:, side_lo:side_lo + side_ref.shape[1]]


def _norm_matmul(x, nw, w, tm, tn, side_start, side_width):
    t, d = x.shape
    n = w.shape[1]
    side_tile, side_lo = divmod(side_start, tn)
    assert side_lo + side_width <= tn
    return pl.pallas_call(
        functools.partial(_norm_matmul_kernel, side_tile=side_tile, side_lo=side_lo),
        grid=(t // tm, n // tn),
        in_specs=[pl.BlockSpec((tm, d), lambda i, j: (i, 0)),
                  pl.BlockSpec((1, d), lambda i, j: (0, 0)),
                  pl.BlockSpec((d, tn), lambda i, j: (0, j))],
        out_specs=[pl.BlockSpec((tm, tn), lambda i, j: (i, j)),
                   pl.BlockSpec((tm, side_width), lambda i, j: (i, 0))],
        out_shape=[jax.ShapeDtypeStruct((t, n), BF16),
                   jax.ShapeDtypeStruct((t, side_width), F32)],
        scratch_shapes=[pltpu.VMEM((tm, d), BF16)],
        compiler_params=_params("parallel", "arbitrary"),
        name="norm_in_proj",
    )(x, nw.reshape(1, d), w)


def _even_proj_kernel(x_ref, nw_ref, w_ref, loglb_ref, log1mlb_ref, o_ref, logf_ref, k_ref, xn_ref,
                      fbuf, *, n_tiles):
    j = pl.program_id(1)
    pl.when(j == 0)(functools.partial(_norm_rows, x_ref, nw_ref, xn_ref))
    acc = _dot(xn_ref[...], w_ref[...])
    o_ref[...] = _bf(acc)

    @pl.when(j == 0)
    def _():
        fbuf[...] = acc

    step_rows = fbuf.shape[0] // n_tiles
    rows = pl.ds(pl.multiple_of(j * step_rows, step_rows), step_rows)
    x = fbuf[rows, :]
    log_sig = jnp.minimum(x, 0.0) - jnp.log1p(jnp.exp(-jnp.abs(x)))
    log_lb = loglb_ref[...]
    b = log1mlb_ref[...] + log_sig
    logf_ref[rows, :] = jnp.maximum(log_lb, b) + jnp.log1p(jnp.exp(-jnp.abs(log_lb - b)))
    k_ref[rows, :] = _bf(jnp.exp(b - x))


def _even_proj(x, nw, w, log_lb, log_1mlb, tm):
    t, d = x.shape
    n = w.shape[1]
    n_tiles = n // d
    return pl.pallas_call(
        functools.partial(_even_proj_kernel, n_tiles=n_tiles),
        grid=(t // tm, n_tiles),
        in_specs=[pl.BlockSpec((tm, d), lambda i, j: (i, 0)),
                  pl.BlockSpec((1, d), lambda i, j: (0, 0)),
                  pl.BlockSpec((d, d), lambda i, j: (0, j)),
                  pl.BlockSpec((1, d), lambda i, j: (0, 0)),
                  pl.BlockSpec((1, d), lambda i, j: (0, 0))],
        out_specs=[pl.BlockSpec((tm, d), lambda i, j: (i, j)),
                   pl.BlockSpec((tm, d), lambda i, j: (i, 0)),
                   pl.BlockSpec((tm, d), lambda i, j: (i, 0))],
        out_shape=[jax.ShapeDtypeStruct((t, n), BF16),
                   jax.ShapeDtypeStruct((t, d), F32),
                   jax.ShapeDtypeStruct((t, d), BF16)],
        scratch_shapes=[pltpu.VMEM((tm, d), BF16), pltpu.VMEM((tm, d), F32)],
        compiler_params=_params("parallel", "arbitrary"),
        name="even_in_proj",
    )(x, nw.reshape(1, d), w, log_lb.reshape(1, d), log_1mlb.reshape(1, d))


def _qkv_conv_kernel(x_ref, nw_ref, w_ref, cw_ref, o_ref, xn_ref, halo_ref, *, tiles_per_seq):
    i = pl.program_id(0)
    j = pl.program_id(1)
    tm, tn = o_ref.shape
    halo = halo_ref.shape[1]

    @pl.when(j == 0)
    def _():
        x = x_ref[...]
        ms = jnp.mean(x * x, axis=-1, keepdims=True)
        xn_ref[...] = _bf(x * lax.rsqrt(ms + RMS_EPS) * nw_ref[...])

    @pl.when(i == 0)
    def _():
        halo_ref[j] = jnp.zeros((halo, tn), F32)

    acc = _dot(xn_ref[...], w_ref[...])
    prev = jnp.where(i % tiles_per_seq == 0, 0.0, halo_ref[j])
    halo_ref[j] = acc[tm - halo:, :]
    ext = jnp.concatenate([prev, acc], axis=0)
    cw = cw_ref[...]
    conv = ext * cw[0:1, :]
    for k in range(1, CONV_K):
        conv = ext * cw[k:k + 1, :] + pltpu.roll(conv, 1, 0)
    y = _silu(conv[halo:, :])

    @pl.when(j < QK_TILES)
    def _():
        scale = jnp.where(j == 0, HEAD ** -0.5, 1.0)
        for h0 in range(0, tn, HEAD):
            yh = y[:, h0:h0 + HEAD]
            inv = lax.rsqrt(jnp.sum(yh * yh, axis=-1, keepdims=True) + RMS_EPS) * scale
            o_ref[:, h0:h0 + HEAD] = _bf(yh * inv)

    @pl.when(j >= QK_TILES)
    def _():
        o_ref[...] = _bf(y)


def _qkv_conv_proj(x, nw, w, conv_w, tm, seq):
    t, d = x.shape
    n = w.shape[1]
    tn = d
    return pl.pallas_call(
        functools.partial(_qkv_conv_kernel, tiles_per_seq=seq // tm),
        grid=(t // tm, n // tn),
        in_specs=[pl.BlockSpec((tm, d), lambda i, j: (i, 0)),
                  pl.BlockSpec((1, d), lambda i, j: (0, 0)),
                  pl.BlockSpec((d, tn), lambda i, j: (0, j)),
                  pl.BlockSpec((CONV_K, tn), lambda i, j: (0, j))],
        out_specs=pl.BlockSpec((tm, tn), lambda i, j: (i, j)),
        out_shape=jax.ShapeDtypeStruct((t, n), BF16),
        scratch_shapes=[pltpu.VMEM((tm, d), BF16),
                        pltpu.VMEM((n // tn, SUBLANES, tn), F32)],
        compiler_params=_params("arbitrary", "arbitrary"),
        name="qkv_conv_proj",
    )(x, nw.reshape(1, d), w, conv_w)


def _out_proj_kernel(*refs, n_in, final):
    h_ref = refs[0]
    a_refs = refs[1:1 + n_in]
    w_refs = refs[1 + n_in:1 + 2 * n_in]
    o_ref = refs[-1]
    acc = h_ref[...]
    for a_ref, w_ref in zip(a_refs, w_refs):
        acc = acc + _dot(a_ref[...], w_ref[...])
    if final:
        fw_ref = refs[1 + 2 * n_in]
        ms = jnp.mean(acc * acc, axis=-1, keepdims=True)
        acc = acc * lax.rsqrt(ms + RMS_EPS) * fw_ref[...]
    o_ref[...] = acc


def _out_proj(h, acts, ws, final_w, tm):
    t, d = h.shape
    n_in = len(acts)
    final = final_w is not None
    in_specs = [pl.BlockSpec((tm, d), lambda i: (i, 0))]
    in_specs += [pl.BlockSpec((tm, a.shape[1]), lambda i: (i, 0)) for a in acts]
    in_specs += [pl.BlockSpec(w.shape, lambda i: (0, 0)) for w in ws]
    args = [h, *acts, *ws]
    if final:
        in_specs.append(pl.BlockSpec((1, d), lambda i: (0, 0)))
        args.append(final_w.reshape(1, d))
    return pl.pallas_call(
        functools.partial(_out_proj_kernel, n_in=n_in, final=final),
        grid=(t // tm,),
        in_specs=in_specs,
        out_specs=pl.BlockSpec((tm, d), lambda i: (i, 0)),
        out_shape=jax.ShapeDtypeStruct((t, d), F32),
        compiler_params=_params("parallel"),
        name="out_proj",
    )(*args)


def _attn_kernel(q_ref, k_ref, v_ref, gate_ref, o_ref, qp_ref, kp_ref, vp_ref, acc_ref, m_ref, l_ref,
                 bias_ref, *, seq):
    per_res = seq // A_MOD
    n_blocks = seq // A_BLOCK
    row = lax.broadcasted_iota(jnp.int32, (A_BLOCK, 2 * A_BLOCK), 0)
    col = lax.broadcasted_iota(jnp.int32, (A_BLOCK, 2 * A_BLOCK), 1)
    lane = lax.broadcasted_iota(jnp.int32, (A_BLOCK, LANES), 1)
    head0 = lane < A_HEAD_DIM

    def widen(t, carry):
        rows = pl.ds(pl.multiple_of(t * A_BLOCK, A_BLOCK), A_BLOCK)
        acc_ref[rows, :] = q_ref[rows, :].astype(F32) * A_SCALE
        m_ref[rows, :] = k_ref[rows, :].astype(F32)
        l_ref[rows, :] = v_ref[rows, :].astype(F32)
        return carry

    lax.fori_loop(0, n_blocks, widen, 0)

    def to_residue_major(res, carry):
        src = pl.ds(res, per_res, stride=A_MOD)
        dst = pl.ds(pl.multiple_of(res * per_res, per_res), per_res)
        qp_ref[dst, :] = acc_ref[src, :]
        kp_ref[dst, :] = m_ref[src, :]
        vp_ref[dst, :] = l_ref[src, :]
        return carry

    lax.fori_loop(0, A_MOD, to_residue_major, 0)

    for dilation in A_DILATIONS:
        first = dilation == A_DILATIONS[0]
        per_class = n_blocks // dilation
        runs = A_MOD // dilation
        rq = A_BLOCK // runs
        dist = runs * (row % rq - col % (2 * rq)) + (row // rq - col // (2 * rq)) + A_BLOCK
        band = (dist >= 0) & (dist <= A_BLOCK)
        in_cur = col % (2 * rq) >= rq
        slot = 2 * A_DILATIONS.index(dilation)
        bias_ref[slot] = jnp.where(band & in_cur, 0.0, -jnp.inf)
        bias_ref[slot + 1] = jnp.where(band, 0.0, -jnp.inf)

        def load(n, dilation=dilation, first=first, per_class=per_class, runs=runs, rq=rq):
            res = n // per_class
            i = n % per_class
            cur, prev = [], []
            for c in range(runs):
                base = pl.multiple_of((c * dilation + res) * per_res + rq * i, SUBLANES)
                cur.append(pl.ds(base, rq))
                prev.append(pl.ds(pl.multiple_of(jnp.where(i > 0, base - rq, base), SUBLANES), rq))

            def window(ref):
                return _bf(jnp.concatenate([ref[rows, :] for pc in zip(prev, cur) for rows in pc], axis=0))

            def block(ref):
                return jnp.concatenate([ref[rows, :] for rows in cur], axis=0)

            return cur, i, block(qp_ref), window(kp_ref), window(vp_ref)

        def scores(i, q, kw, vw, slot=slot):
            bias = bias_ref[slot + (i > 0).astype(jnp.int32)]
            parts = []
            for h in range(2):
                qh = _bf(jnp.where(head0 if h == 0 else ~head0, q, 0.0))
                s = _dot_nt(qh, kw) + bias
                m_blk = jnp.max(s, axis=-1, keepdims=True)
                p = jnp.exp2(s - m_blk)
                parts.append((_dot(_bf(p), vw), m_blk, jnp.sum(p, axis=-1, keepdims=True)))
            return tuple(jnp.where(head0, a, b) for a, b in zip(*parts))

        def group(t, carry, load=load, scores=scores, first=first, rq=rq):
            loaded = [load(t * A_GROUP + u) for u in range(A_GROUP)]
            results = [scores(*item[1:]) for item in loaded]
            for (cur, *_), (acc, m_blk, l_blk) in zip(loaded, results):
                if not first:
                    acc_old, m_old, l_old = (
                        jnp.concatenate([ref[rows, :] for rows in cur], axis=0)
                        for ref in (acc_ref, m_ref, l_ref))
                    m_new = jnp.maximum(m_old, m_blk)
                    alpha = jnp.exp2(m_old - m_new)
                    beta = jnp.exp2(m_blk - m_new)
                    acc = acc_old * alpha + acc * beta
                    l_blk = l_old * alpha + l_blk * beta
                    m_blk = m_new
                for ref, val in zip((acc_ref, m_ref, l_ref), (acc, m_blk, l_blk)):
                    for c, rows in enumerate(cur):
                        ref[rows, :] = val[c * rq:(c + 1) * rq, :]
            return carry

        lax.fori_loop(0, n_blocks // A_GROUP, group, 0)

    def to_position_order(res, carry):
        src = pl.ds(pl.multiple_of(res * per_res, per_res), per_res)
        m_ref[pl.ds(res, per_res, stride=A_MOD), :] = acc_ref[src, :] / l_ref[src, :]
        return carry

    lax.fori_loop(0, A_MOD, to_position_order, 0)

    def finish(t, carry):
        rows = pl.ds(pl.multiple_of(t * A_BLOCK, A_BLOCK), A_BLOCK)
        o_ref[rows, :] = _bf(m_ref[rows, :] * _silu(gate_ref[rows, :].astype(F32)))
        return carry

    lax.fori_loop(0, n_blocks, finish, 0)


def _dilated_attention(proj):
    bsz, s, _ = proj.shape
    hp_blocks = D_MODEL // LANES

    def col_spec(off):
        return pl.BlockSpec((None, s, LANES), lambda b, hp: (b, 0, (off + 1) * hp_blocks + hp))

    return pl.pallas_call(
        functools.partial(_attn_kernel, seq=s),
        grid=(bsz, hp_blocks),
        in_specs=[col_spec(0), col_spec(1), col_spec(2), col_spec(3)],
        out_specs=pl.BlockSpec((None, s, LANES), lambda b, hp: (b, 0, hp)),
        out_shape=jax.ShapeDtypeStruct((bsz, s, D_MODEL), BF16),
        scratch_shapes=[pltpu.VMEM((s, LANES), F32)] * 6
        + [pltpu.VMEM((2 * len(A_DILATIONS), A_BLOCK, 2 * A_BLOCK), F32)],
        compiler_params=_params("parallel", "parallel"),
        name="dilated_attn",
    )(proj, proj, proj, proj)


def _hgrn_kernel(q_ref, logf_ref, k_ref, i_ref, gate_ref, nw_ref, o_ref, st_ref, *, tb):
    @pl.when(pl.program_id(2) == 0)
    def _():
        st_ref[...] = jnp.zeros_like(st_ref)

    n_ch = tb // CHUNK
    r64 = lax.broadcasted_iota(jnp.int32, (CHUNK, CHUNK), 0)
    c64 = lax.broadcasted_iota(jnp.int32, (CHUNK, CHUNK), 1)
    tri = _bf(jnp.where(r64 >= c64, 1.0, 0.0))
    row = lax.broadcasted_iota(jnp.int32, (CHUNK, HEAD), 0)
    sub = lax.broadcasted_iota(jnp.int32, (SUBLANES, HEAD), 0)
    units = [(hd, slice(c * CHUNK, (c + 1) * CHUNK), slice(hd * HEAD, (hd + 1) * HEAD))
             for c in range(n_ch) for hd in range(HGRN_KH)]
    n_units = len(units)
    qs = [q_ref[rows, cols].astype(F32) for _, rows, cols in units]
    ks = [k_ref[rows, cols].astype(F32) for _, rows, cols in units]
    vbs = [i_ref[rows, cols] for _, rows, cols in units]

    gs = []
    for _, rows, cols in units:
        lf = logf_ref[rows, cols]
        hi = _bf(lf)
        r1 = lf - hi.astype(F32)
        mid = _bf(r1)
        lo = _bf(r1 - mid.astype(F32))
        cum = _dot(tri, jnp.concatenate([hi, mid, lo], axis=1))
        gs.append(cum[:, :HEAD] + cum[:, HEAD:2 * HEAD] + cum[:, 2 * HEAD:])

    attn = [None] * n_units
    s = CHUNK // 2
    while s >= 1:
        blk = 2 * s
        upper = (row % blk) >= s
        mask = ((r64 // blk) == (c64 // blk)) & ((r64 % blk) >= s) & ((c64 % blk) < s)
        for c in range(n_units):
            g = gs[c]
            if blk >= SUBLANES:
                g_ref = jnp.concatenate(
                    [jnp.broadcast_to(g[b0 + s - 1:b0 + s, :], (blk, HEAD)) for b0 in range(0, CHUNK, blk)],
                    axis=0)
            elif s == 2:
                g_ref = jnp.concatenate(
                    [jnp.where(sub < 4, jnp.broadcast_to(g[v0 + 1:v0 + 2, :], (SUBLANES, HEAD)),
                               jnp.broadcast_to(g[v0 + 5:v0 + 6, :], (SUBLANES, HEAD)))
                     for v0 in range(0, CHUNK, SUBLANES)], axis=0)
            else:
                g_ref = jnp.where(upper, pltpu.roll(g, 1, 0), g)
            z = _bf(jnp.where(upper, qs[c], ks[c]) * jnp.exp(-jnp.abs(g - g_ref)))
            m = _dot_nt(z, z)
            attn[c] = jnp.where(mask, m, 0.0) if attn[c] is None else jnp.where(mask, m, attn[c])
        s //= 2
    eye = r64 == c64
    attn = [jnp.where(eye, jnp.sum(qs[c] * ks[c], axis=-1, keepdims=True), attn[c]) for c in range(n_units)]

    o_intra = [_dot(_bf(attn[c]), vbs[c]) for c in range(n_units)]
    q_dec = [_bf(qs[c] * jnp.exp(gs[c])) for c in range(n_units)]
    g_last = [gs[c][CHUNK - 1:CHUNK, :] for c in range(n_units)]
    kv = [_dot_tn(vbs[c], _bf(ks[c] * jnp.exp(g_last[c] - gs[c]))) for c in range(n_units)]

    states = [st_ref[hd] for hd in range(HGRN_KH)]
    outs = []
    for c, (hd, _, _) in enumerate(units):
        outs.append(_dot_nt(q_dec[c], _bf(states[hd])) + o_intra[c])
        states[hd] = states[hd] * jnp.exp(g_last[c]) + kv[c]
    for hd in range(HGRN_KH):
        st_ref[hd] = states[hd]

    for c, (_, rows, cols) in enumerate(units):
        o = outs[c]
        ms = jnp.mean(o * o, axis=-1, keepdims=True)
        o = o * lax.rsqrt(ms + RMS_EPS) * nw_ref[...]
        o_ref[rows, cols] = _bf(o * _silu(gate_ref[rows, cols].astype(F32)))


def _hgrn(proj, log_f, k, norm_w, tb):
    bsz, s, _ = proj.shape
    width = HGRN_KH * HEAD
    groups = D_MODEL // width
    base = 5 * groups

    def col_spec(off):
        return pl.BlockSpec((None, tb, width), lambda b, h, t: (b, t, base + off * groups + h))

    head_spec = pl.BlockSpec((None, tb, width), lambda b, h, t: (b, t, h))
    return pl.pallas_call(
        functools.partial(_hgrn_kernel, tb=tb),
        grid=(bsz, groups, s // tb),
        in_specs=[col_spec(0), head_spec, head_spec, col_spec(1), col_spec(2),
                  pl.BlockSpec((1, HEAD), lambda b, h, t: (0, 0))],
        out_specs=head_spec,
        out_shape=jax.ShapeDtypeStruct((bsz, s, D_MODEL), BF16),
        scratch_shapes=[pltpu.VMEM((HGRN_KH, HEAD, HEAD), F32)],
        compiler_params=_params("parallel", "parallel", "arbitrary"),
        name="hgrn2",
    )(proj, log_f, k, proj, proj, norm_w.reshape(1, HEAD))


def _unit_lower_inverses(lows):
    r = lax.broadcasted_iota(jnp.int32, (CHUNK, CHUNK), 0)
    c = lax.broadcasted_iota(jnp.int32, (CHUNK, CHUNK), 1)
    eye = jnp.where(r == c, 1.0, 0.0).astype(F32)
    s = 1
    invs = None
    while s < CHUNK:
        sel = ((r // (2 * s)) == (c // (2 * s))) & ((r % (2 * s)) >= s) & ((c % (2 * s)) < s)
        low_s = [jnp.where(sel, low, 0.0) for low in lows]
        if s == 1:
            invs = [eye - x for x in low_s]
        else:
            inv_b = [_bf(x) for x in invs]
            tmp = [_bf(_dot(a, _bf(x))) for a, x in zip(inv_b, low_s)]
            invs = [x - _dot(t, a) for x, t, a in zip(invs, tmp, inv_b)]
        s *= 2
    return invs


def _gdn_kernel(q_ref, k_ref, v_ref, z_ref, gates_ref, avec_ref, dtb_ref, nw_ref, o_ref, st_ref, *, tb):
    tt = pl.program_id(2)
    group = pl.program_id(1)
    n_ch = tb // CHUNK

    @pl.when(tt == 0)
    def _():
        st_ref[...] = jnp.zeros_like(st_ref)

    q_all = [q_ref[:, kk * HEAD:(kk + 1) * HEAD].astype(F32) for kk in range(GDN_KH)]
    k_all = [k_ref[:, kk * HEAD:(kk + 1) * HEAD].astype(F32) for kk in range(GDN_KH)]

    gl = gates_ref[...]
    beta_all = jax.nn.sigmoid(gl)
    g_all = avec_ref[...] * _softplus(gl + dtb_ref[...])
    lane = lax.broadcasted_iota(jnp.int32, (tb, LANES), 1)

    r64 = lax.broadcasted_iota(jnp.int32, (CHUNK, CHUNK), 0)
    c64 = lax.broadcasted_iota(jnp.int32, (CHUNK, CHUNK), 1)
    tri = _bf(jnp.where(r64 >= c64, 1.0, 0.0))
    causal = r64 >= c64
    strict = r64 > c64
    strict_f = jnp.where(strict, 1.0, 0.0).astype(F32)

    chunk_rows = [slice(c * CHUNK, (c + 1) * CHUNK) for c in range(n_ch)]
    n_vh = 2 * GDN_KH
    units = [(vh, c) for c in range(n_ch) for vh in range(n_vh)]

    beta_cols, g_cols = [], []
    for vh in range(n_vh):
        head = n_vh * group + vh
        beta_cols.append(jnp.sum(jnp.where(lane == head, beta_all, 0.0), axis=-1, keepdims=True))
        g_cols.append(jnp.sum(jnp.where(lane == N_V_HEADS + head, g_all, 0.0), axis=-1, keepdims=True))

    kq = [[_dot_nt(_bf(jnp.concatenate([k_all[kk][rows, :], q_all[kk][rows, :]], axis=0)),
                   _bf(k_all[kk][rows, :])) for rows in chunk_rows]
          for kk in range(GDN_KH)]

    decay, g_i, g_last = [], [], []
    for vh, c in units:
        gb = jnp.broadcast_to(g_cols[vh][chunk_rows[c], :], (CHUNK, CHUNK))
        wmat = jnp.concatenate([gb * strict_f, gb], axis=1)
        hi = _bf(wmat)
        lo = _bf(wmat - hi.astype(F32))
        cum = _dot(tri, hi) + _dot(tri, lo)
        decay.append(jnp.where(causal, jnp.exp(jnp.minimum(cum[:, :CHUNK], 0.0)), 0.0))
        g_i.append(cum[:, CHUNK:CHUNK + 1])
        g_last.append(cum[CHUNK - 1:CHUNK, CHUNK:CHUNK + 1])

    lows = [jnp.where(strict, kq[vh // 2][c][:CHUNK, :] * beta_cols[vh][chunk_rows[c], :] * decay[n], 0.0)
            for n, (vh, c) in enumerate(units)]
    invs = _unit_lower_inverses(lows)

    sols = []
    for n, (vh, c) in enumerate(units):
        rows = chunk_rows[c]
        beta = beta_cols[vh][rows, :]
        v = v_ref[rows, vh * HEAD:(vh + 1) * HEAD].astype(F32)
        rhs = jnp.concatenate([v * beta, k_all[vh // 2][rows, :] * (beta * jnp.exp(g_i[n]))], axis=1)
        sols.append(_bf(_dot(_bf(invs[n]), _bf(rhs))))

    o_loc, q_eff, c_mat, p_mat = [], [], [], []
    for n, (vh, c) in enumerate(units):
        rows = chunk_rows[c]
        attn = _bf(kq[vh // 2][c][CHUNK:, :] * decay[n])
        auw = _dot(attn, sols[n])
        o_loc.append(auw[:, :HEAD])
        q_eff.append(_bf(q_all[vh // 2][rows, :] * jnp.exp(g_i[n]) - auw[:, HEAD:]))
        k_tail = _bf(k_all[vh // 2][rows, :] * jnp.exp(g_last[n] - g_i[n]))
        ktuw = _dot_tn(k_tail, sols[n])
        c_mat.append(ktuw[:, :HEAD])
        p_mat.append(_bf(ktuw[:, HEAD:]))

    states = [st_ref[vh] for vh in range(n_vh)]
    outs = [None] * len(units)
    for n, (vh, c) in enumerate(units):
        sb = _bf(states[vh])
        outs[n] = _dot(q_eff[n], sb) + o_loc[n]
        states[vh] = states[vh] * jnp.exp(g_last[n]) + c_mat[n] - _dot(p_mat[n], sb)
    for vh in range(n_vh):
        st_ref[vh] = states[vh]

    for n, (vh, c) in enumerate(units):
        rows = chunk_rows[c]
        o = outs[n]
        ms = jnp.mean(o * o, axis=-1, keepdims=True)
        o = o * lax.rsqrt(ms + RMS_EPS) * nw_ref[...]
        z = z_ref[rows, vh * HEAD:(vh + 1) * HEAD].astype(F32)
        o_ref[rows, vh * HEAD:(vh + 1) * HEAD] = _bf(o * _silu(z))


def _gdn(qkv, z, gate_logits, a_vec, dtb_vec, norm_w, tb):
    bsz, s, _ = qkv.shape
    width = GDN_KH * HEAD
    groups = D_MODEL // width

    in_specs = [
        pl.BlockSpec((None, tb, width), lambda b, h, t: (b, t, h)),
        pl.BlockSpec((None, tb, width), lambda b, h, t: (b, t, groups + h)),
        pl.BlockSpec((None, tb, 2 * width), lambda b, h, t: (b, t, groups + h)),
        pl.BlockSpec((None, tb, 2 * width), lambda b, h, t: (b, t, h)),
        pl.BlockSpec((None, tb, LANES), lambda b, h, t: (b, t, 0)),
        pl.BlockSpec((1, LANES), lambda b, h, t: (0, 0)),
        pl.BlockSpec((1, LANES), lambda b, h, t: (0, 0)),
        pl.BlockSpec((1, HEAD), lambda b, h, t: (0, 0)),
    ]
    return pl.pallas_call(
        functools.partial(_gdn_kernel, tb=tb),
        grid=(bsz, groups, s // tb),
        in_specs=in_specs,
        out_specs=pl.BlockSpec((None, tb, 2 * width), lambda b, h, t: (b, t, h)),
        out_shape=jax.ShapeDtypeStruct((bsz, s, 2 * D_MODEL), BF16),
        scratch_shapes=[pltpu.VMEM((2 * GDN_KH, HEAD, HEAD), F32)],
        compiler_params=_params("parallel", "parallel", "arbitrary"),
        name="gated_deltanet",
    )(qkv, qkv, qkv, z, gate_logits, a_vec, dtb_vec, norm_w.reshape(1, HEAD))


def kernel(x, norm_w, final_norm_w, even_w_in, even_w_out, hgrn_lb_logits, hgrn_norm_w,
           odd_w_in, odd_conv_w, odd_dt_bias, odd_a_log, odd_norm_w, odd_w_out):
    bsz, s, d = x.shape
    depth = norm_w.shape[0]
    t = bsz * s
    tm = 512
    tm_in = 1024
    tb = 512

    lb_all = jnp.cumsum(jax.nn.softmax(hgrn_lb_logits.astype(F32), axis=0), axis=0)
    lb_all = jnp.maximum(lb_all - lb_all[0:1], 0.0)
    log_lb = jnp.log(lb_all)
    log_1mlb = jnp.log1p(-lb_all)

    h = x.reshape(t, d)
    for layer in range(depth):
        j = layer // 2
        final_w = final_norm_w if layer == depth - 1 else None
        if layer % 2 == 0:
            w = even_w_in[j]
            w = _bf(jnp.concatenate([w[:, 5 * d:6 * d], w[:, :5 * d], w[:, 6 * d:]], axis=1))
            proj, log_f, k = _even_proj(h, norm_w[layer], w, log_lb[j], log_1mlb[j], tm_in)
            proj = proj.reshape(bsz, s, EVEN_IN)
            a_mix = _dilated_attention(proj)
            b_mix = _hgrn(proj, log_f.reshape(bsz, s, d), k.reshape(bsz, s, d), hgrn_norm_w[j], tb)
            w_out = _bf(even_w_out[j])
            h = _out_proj(h, [a_mix.reshape(t, d), b_mix.reshape(t, d)], [w_out[:d], w_out[d:]],
                          final_w, tm)
        else:
            n_rest = odd_w_in.shape[2] - ODD_CONV
            pad = jnp.zeros((d, ODD_REST_PAD - n_rest), F32)
            w_qkv = _bf(odd_w_in[j][:, :ODD_CONV])
            w_rest = _bf(jnp.concatenate([odd_w_in[j][:, ODD_CONV:], pad], axis=1))
            qkv = _qkv_conv_proj(h, norm_w[layer], w_qkv, odd_conv_w[j], tm_in, s)
            z, gate_logits = _norm_matmul(h, norm_w[layer], w_rest, tm_in, ODD_REST_PAD,
                                          side_start=ODD_REST_PAD - LANES, side_width=LANES)
            zeros16 = jnp.zeros((N_V_HEADS,), F32)
            tail = jnp.zeros((LANES - 2 * N_V_HEADS,), F32)
            a_vec = jnp.concatenate([zeros16, -jnp.exp(odd_a_log[j].astype(F32)), tail]).reshape(1, LANES)
            dtb_vec = jnp.concatenate([zeros16, odd_dt_bias[j].astype(F32), tail]).reshape(1, LANES)
            o = _gdn(qkv.reshape(bsz, s, ODD_CONV), z.reshape(bsz, s, ODD_REST_PAD),
                     gate_logits.reshape(bsz, s, LANES), a_vec, dtb_vec, odd_norm_w[j], tb)
            h = _out_proj(h, [o.reshape(t, 2 * d)], [_bf(odd_w_out[j])], final_w, tm)
    return h.reshape(bsz, s, d)
```

```python
import functools

import jax
import jax.numpy as jnp
from jax import lax
from jax.experimental import pallas as pl
from jax.experimental.pallas import tpu as pltpu

F32 = jnp.float32
BF16 = jnp.bfloat16

RMS_EPS = 1e-6
LANES = 128
SUBLANES = 8
VMEM_LIMIT = 48 * 1024 * 1024

D_MODEL = 1024
A_HEAD_DIM = 64
A_BLOCK = 128
A_DILATIONS = (1, 4, 16)
CHUNK = 64
A_SCALE = A_HEAD_DIM ** -0.5 * 1.4426950408889634
A_MOD = 16
A_GROUP = 8
HEAD = 128
CONV_K = 4
HGRN_KH = 2
GDN_KH = 2

EVEN_IN = 8 * D_MODEL
ODD_CONV = 4 * D_MODEL
QK_TILES = 2
ODD_REST_PAD = 2 * D_MODEL + LANES
N_V_HEADS = 16


def _dot(a, b):
    return jnp.dot(a, b, preferred_element_type=F32)


def _dot_nt(a, b, precision=None):
    return lax.dot_general(a, b, (((1,), (1,)), ((), ())), precision=precision,
                           preferred_element_type=F32)


def _dot_tn(a, b):
    return lax.dot_general(a, b, (((0,), (0,)), ((), ())), preferred_element_type=F32)


def _bf(x):
    return x.astype(BF16)


def _silu(x):
    return x * jax.nn.sigmoid(x)


def _softplus(x):
    return jnp.maximum(x, 0.0) + jnp.log1p(jnp.exp(-jnp.abs(x)))


def _params(*sem):
    return pltpu.CompilerParams(dimension_semantics=sem, vmem_limit_bytes=VMEM_LIMIT)


def _norm_rows(x_ref, nw_ref, xn_ref):
    x = x_ref[...]
    ms = jnp.mean(x * x, axis=-1, keepdims=True)
    xn_ref[...] = _bf(x * lax.rsqrt(ms + RMS_EPS) * nw_ref[...])


def _norm_matmul_kernel(x_ref, nw_ref, w_ref, o_ref, side_ref, xn_ref, *, side_tile, side_lo):
    j = pl.program_id(1)
    pl.when(j == 0)(functools.partial(_norm_rows, x_ref, nw_ref, xn_ref))
    acc = _dot(xn_ref[...], w_ref[...])
    o_ref[...] = _bf(acc)

    @pl.when(j == side_tile)
    def _():
        side_ref[...] = acc[:, side_lo:side_lo + side_ref.shape[1]]


def _norm_matmul(x, nw, w, tm, tn, side_start, side_width):
    t, d = x.shape
    n = w.shape[1]
    side_tile, side_lo = divmod(side_start, tn)
    assert side_lo + side_width <= tn
    return pl.pallas_call(
        functools.partial(_norm_matmul_kernel, side_tile=side_tile, side_lo=side_lo),
        grid=(t // tm, n // tn),
        in_specs=[pl.BlockSpec((tm, d), lambda i, j: (i, 0)),
                  pl.BlockSpec((1, d), lambda i, j: (0, 0)),
                  pl.BlockSpec((d, tn), lambda i, j: (0, j))],
        out_specs=[pl.BlockSpec((tm, tn), lambda i, j: (i, j)),
                   pl.BlockSpec((tm, side_width), lambda i, j: (i, 0))],
        out_shape=[jax.ShapeDtypeStruct((t, n), BF16),
                   jax.ShapeDtypeStruct((t, side_width), F32)],
        scratch_shapes=[pltpu.VMEM((tm, d), BF16)],
        compiler_params=_params("parallel", "arbitrary"),
        name="norm_in_proj",
    )(x, nw.reshape(1, d), w)


def _even_proj_kernel(x_ref, nw_ref, w_ref, loglb_ref, log1mlb_ref, o_ref, logf_ref, k_ref, xn_ref,
                      fbuf, *, n_tiles):
    j = pl.program_id(1)
    pl.when(j == 0)(functools.partial(_norm_rows, x_ref, nw_ref, xn_ref))
    acc = _dot(xn_ref[...], w_ref[...])
    o_ref[...] = _bf(acc)

    @pl.when(j == 0)
    def _():
        fbuf[...] = acc

    step_rows = fbuf.shape[0] // n_tiles
    rows = pl.ds(pl.multiple_of(j * step_rows, step_rows), step_rows)
    x = fbuf[rows, :]
    log_sig = jnp.minimum(x, 0.0) - jnp.log1p(jnp.exp(-jnp.abs(x)))
    log_lb = loglb_ref[...]
    b = log1mlb_ref[...] + log_sig
    logf_ref[rows, :] = jnp.maximum(log_lb, b) + jnp.log1p(jnp.exp(-jnp.abs(log_lb - b)))
    k_ref[rows, :] = _bf(jnp.exp(b - x))


def _even_proj(x, nw, w, log_lb, log_1mlb, tm):
    t, d = x.shape
    n = w.shape[1]
    n_tiles = n // d
    return pl.pallas_call(
        functools.partial(_even_proj_kernel, n_tiles=n_tiles),
        grid=(t // tm, n_tiles),
        in_specs=[pl.BlockSpec((tm, d), lambda i, j: (i, 0)),
                  pl.BlockSpec((1, d), lambda i, j: (0, 0)),
                  pl.BlockSpec((d, d), lambda i, j: (0, j)),
                  pl.BlockSpec((1, d), lambda i, j: (0, 0)),
                  pl.BlockSpec((1, d), lambda i, j: (0, 0))],
        out_specs=[pl.BlockSpec((tm, d), lambda i, j: (i, j)),
                   pl.BlockSpec((tm, d), lambda i, j: (i, 0)),
                   pl.BlockSpec((tm, d), lambda i, j: (i, 0))],
        out_shape=[jax.ShapeDtypeStruct((t, n), BF16),
                   jax.ShapeDtypeStruct((t, d), F32),
                   jax.ShapeDtypeStruct((t, d), BF16)],
        scratch_shapes=[pltpu.VMEM((tm, d), BF16), pltpu.VMEM((tm, d), F32)],
        compiler_params=_params("parallel", "arbitrary"),
        name="even_in_proj",
    )(x, nw.reshape(1, d), w, log_lb.reshape(1, d), log_1mlb.reshape(1, d))


def _qkv_conv_kernel(x_ref, nw_ref, w_ref, cw_ref, o_ref, xn_ref, halo_ref, *, tiles_per_seq):
    i = pl.program_id(0)
    j = pl.program_id(1)
    tm, tn = o_ref.shape
    halo = halo_ref.shape[1]

    pl.when(j == 0)(functools.partial(_norm_rows, x_ref, nw_ref, xn_ref))

    @pl.when(i == 0)
    def _():
        halo_ref[j] = jnp.zeros((halo, tn), F32)

    acc = _dot(xn_ref[...], w_ref[...])
    prev = jnp.where(i % tiles_per_seq == 0, 0.0, halo_ref[j])
    halo_ref[j] = acc[tm - halo:, :]
    ext = jnp.concatenate([prev, acc], axis=0)
    cw = cw_ref[...]
    conv = ext * cw[0:1, :]
    for k in range(1, CONV_K):
        conv = ext * cw[k:k + 1, :] + pltpu.roll(conv, 1, 0)
    y = _silu(conv[halo:, :])

    @pl.when(j < QK_TILES)
    def _():
        scale = jnp.where(j == 0, HEAD ** -0.5, 1.0)
        for h0 in range(0, tn, HEAD):
            yh = y[:, h0:h0 + HEAD]
            inv = lax.rsqrt(jnp.sum(yh * yh, axis=-1, keepdims=True) + RMS_EPS) * scale
            o_ref[:, h0:h0 + HEAD] = _bf(yh * inv)

    @pl.when(j >= QK_TILES)
    def _():
        o_ref[...] = _bf(y)


def _qkv_conv_proj(x, nw, w, conv_w, tm, seq):
    t, d = x.shape
    n = w.shape[1]
    tn = d
    return pl.pallas_call(
        functools.partial(_qkv_conv_kernel, tiles_per_seq=seq // tm),
        grid=(t // tm, n // tn),
        in_specs=[pl.BlockSpec((tm, d), lambda i, j: (i, 0)),
                  pl.BlockSpec((1, d), lambda i, j: (0, 0)),
                  pl.BlockSpec((d, tn), lambda i, j: (0, j)),
                  pl.BlockSpec((CONV_K, tn), lambda i, j: (0, j))],
        out_specs=pl.BlockSpec((tm, tn), lambda i, j: (i, j)),
        out_shape=jax.ShapeDtypeStruct((t, n), BF16),
        scratch_shapes=[pltpu.VMEM((tm, d), BF16),
                        pltpu.VMEM((n // tn, SUBLANES, tn), F32)],
        compiler_params=_params("arbitrary", "arbitrary"),
        name="qkv_conv_proj",
    )(x, nw.reshape(1, d), w, conv_w)


def _out_proj_kernel(*refs, n_in, final):
    h_ref = refs[0]
    a_refs = refs[1:1 + n_in]
    w_refs = refs[1 + n_in:1 + 2 * n_in]
    o_ref = refs[-1]
    acc = h_ref[...]
    for a_ref, w_ref in zip(a_refs, w_refs):
        acc = acc + _dot(a_ref[...], w_ref[...])
    if final:
        fw_ref = refs[1 + 2 * n_in]
        ms = jnp.mean(acc * acc, axis=-1, keepdims=True)
        acc = acc * lax.rsqrt(ms + RMS_EPS) * fw_ref[...]
    o_ref[...] = acc


def _out_proj(h, acts, ws, final_w, tm):
    t, d = h.shape
    n_in = len(acts)
    final = final_w is not None
    in_specs = [pl.BlockSpec((tm, d), lambda i: (i, 0))]
    in_specs += [pl.BlockSpec((tm, a.shape[1]), lambda i: (i, 0)) for a in acts]
    in_specs += [pl.BlockSpec(w.shape, lambda i: (0, 0)) for w in ws]
    args = [h, *acts, *ws]
    if final:
        in_specs.append(pl.BlockSpec((1, d), lambda i: (0, 0)))
        args.append(final_w.reshape(1, d))
    return pl.pallas_call(
        functools.partial(_out_proj_kernel, n_in=n_in, final=final),
        grid=(t // tm,),
        in_specs=in_specs,
        out_specs=pl.BlockSpec((tm, d), lambda i: (i, 0)),
        out_shape=jax.ShapeDtypeStruct((t, d), F32),
        compiler_params=_params("parallel"),
        name="out_proj",
    )(*args)


def _attn_kernel(q_ref, k_ref, v_ref, gate_ref, o_ref, qp_ref, kp_ref, vp_ref, acc_ref, m_ref, l_ref,
                 bias_ref, *, seq):
    per_res = seq // A_MOD
    n_blocks = seq // A_BLOCK
    row = lax.broadcasted_iota(jnp.int32, (A_BLOCK, 2 * A_BLOCK), 0)
    col = lax.broadcasted_iota(jnp.int32, (A_BLOCK, 2 * A_BLOCK), 1)
    lane = lax.broadcasted_iota(jnp.int32, (A_BLOCK, LANES), 1)
    head0 = lane < A_HEAD_DIM

    def widen(t, carry):
        rows = pl.ds(pl.multiple_of(t * A_BLOCK, A_BLOCK), A_BLOCK)
        acc_ref[rows, :] = q_ref[rows, :].astype(F32) * A_SCALE
        m_ref[rows, :] = k_ref[rows, :].astype(F32)
        l_ref[rows, :] = v_ref[rows, :].astype(F32)
        return carry

    lax.fori_loop(0, n_blocks, widen, 0)

    def to_residue_major(res, carry):
        src = pl.ds(res, per_res, stride=A_MOD)
        dst = pl.ds(pl.multiple_of(res * per_res, per_res), per_res)
        qp_ref[dst, :] = acc_ref[src, :]
        kp_ref[dst, :] = m_ref[src, :]
        vp_ref[dst, :] = l_ref[src, :]
        return carry

    lax.fori_loop(0, A_MOD, to_residue_major, 0)

    for dilation in A_DILATIONS:
        first = dilation == A_DILATIONS[0]
        per_class = n_blocks // dilation
        runs = A_MOD // dilation
        rq = A_BLOCK // runs
        dist = runs * (row % rq - col % (2 * rq)) + (row // rq - col // (2 * rq)) + A_BLOCK
        band = (dist >= 0) & (dist <= A_BLOCK)
        in_cur = col % (2 * rq) >= rq
        slot = 2 * A_DILATIONS.index(dilation)
        bias_ref[slot] = jnp.where(band & in_cur, 0.0, -jnp.inf)
        bias_ref[slot + 1] = jnp.where(band, 0.0, -jnp.inf)

        def load(n, dilation=dilation, first=first, per_class=per_class, runs=runs, rq=rq):
            res = n // per_class
            i = n % per_class
            cur, prev = [], []
            for c in range(runs):
                base = pl.multiple_of((c * dilation + res) * per_res + rq * i, SUBLANES)
                cur.append(pl.ds(base, rq))
                prev.append(pl.ds(pl.multiple_of(jnp.where(i > 0, base - rq, base), SUBLANES), rq))

            def window(ref):
                return _bf(jnp.concatenate([ref[rows, :] for pc in zip(prev, cur) for rows in pc], axis=0))

            def block(ref):
                return jnp.concatenate([ref[rows, :] for rows in cur], axis=0)

            return cur, i, block(qp_ref), window(kp_ref), window(vp_ref)

        def scores(i, q, kw, vw, slot=slot):
            bias = bias_ref[slot + (i > 0).astype(jnp.int32)]
            parts = []
            for h in range(2):
                qh = _bf(jnp.where(head0 if h == 0 else ~head0, q, 0.0))
                s = _dot_nt(qh, kw) + bias
                m_blk = jnp.max(s, axis=-1, keepdims=True)
                p = jnp.exp2(s - m_blk)
                parts.append((_dot(_bf(p), vw), m_blk, jnp.sum(p, axis=-1, keepdims=True)))
            return tuple(jnp.where(head0, a, b) for a, b in zip(*parts))

        def group(t, carry, load=load, scores=scores, first=first, rq=rq):
            loaded = [load(t * A_GROUP + u) for u in range(A_GROUP)]
            results = [scores(*item[1:]) for item in loaded]
            for (cur, *_), (acc, m_blk, l_blk) in zip(loaded, results):
                if not first:
                    acc_old, m_old, l_old = (
                        jnp.concatenate([ref[rows, :] for rows in cur], axis=0)
                        for ref in (acc_ref, m_ref, l_ref))
                    m_new = jnp.maximum(m_old, m_blk)
                    alpha = jnp.exp2(m_old - m_new)
                    beta = jnp.exp2(m_blk - m_new)
                    acc = acc_old * alpha + acc * beta
                    l_blk = l_old * alpha + l_blk * beta
                    m_blk = m_new
                for ref, val in zip((acc_ref, m_ref, l_ref), (acc, m_blk, l_blk)):
                    for c, rows in enumerate(cur):
                        ref[rows, :] = val[c * rq:(c + 1) * rq, :]
            return carry

        lax.fori_loop(0, n_blocks // A_GROUP, group, 0)

    def to_position_order(res, carry):
        src = pl.ds(pl.multiple_of(res * per_res, per_res), per_res)
        m_ref[pl.ds(res, per_res, stride=A_MOD), :] = acc_ref[src, :] / l_ref[src, :]
        return carry

    lax.fori_loop(0, A_MOD, to_position_order, 0)

    def finish(t, carry):
        rows = pl.ds(pl.multiple_of(t * A_BLOCK, A_BLOCK), A_BLOCK)
        o_ref[rows, :] = _bf(m_ref[rows, :] * _silu(gate_ref[rows, :].astype(F32)))
        return carry

    lax.fori_loop(0, n_blocks, finish, 0)


def _dilated_attention(proj):
    bsz, s, _ = proj.shape
    hp_blocks = D_MODEL // LANES

    def col_spec(off):
        return pl.BlockSpec((None, s, LANES), lambda b, hp: (b, 0, (off + 1) * hp_blocks + hp))

    return pl.pallas_call(
        functools.partial(_attn_kernel, seq=s),
        grid=(bsz, hp_blocks),
        in_specs=[col_spec(0), col_spec(1), col_spec(2), col_spec(3)],
        out_specs=pl.BlockSpec((None, s, LANES), lambda b, hp: (b, 0, hp)),
        out_shape=jax.ShapeDtypeStruct((bsz, s, D_MODEL), BF16),
        scratch_shapes=[pltpu.VMEM((s, LANES), F32)] * 6
        + [pltpu.VMEM((2 * len(A_DILATIONS), A_BLOCK, 2 * A_BLOCK), F32)],
        compiler_params=_params("parallel", "parallel"),
        name="dilated_attn",
    )(proj, proj, proj, proj)


def _hgrn_kernel(q_ref, logf_ref, k_ref, i_ref, gate_ref, nw_ref, o_ref, st_ref, *, tb):
    @pl.when(pl.program_id(2) == 0)
    def _():
        st_ref[...] = jnp.zeros_like(st_ref)

    n_ch = tb // CHUNK
    r64 = lax.broadcasted_iota(jnp.int32, (CHUNK, CHUNK), 0)
    c64 = lax.broadcasted_iota(jnp.int32, (CHUNK, CHUNK), 1)
    tri = _bf(jnp.where(r64 >= c64, 1.0, 0.0))
    row = lax.broadcasted_iota(jnp.int32, (CHUNK, HEAD), 0)
    sub = lax.broadcasted_iota(jnp.int32, (SUBLANES, HEAD), 0)
    units = [(hd, slice(c * CHUNK, (c + 1) * CHUNK), slice(hd * HEAD, (hd + 1) * HEAD))
             for c in range(n_ch) for hd in range(HGRN_KH)]
    n_units = len(units)
    qs = [q_ref[rows, cols].astype(F32) for _, rows, cols in units]
    ks = [k_ref[rows, cols].astype(F32) for _, rows, cols in units]
    vbs = [i_ref[rows, cols] for _, rows, cols in units]

    gs = []
    for _, rows, cols in units:
        lf = logf_ref[rows, cols]
        hi = _bf(lf)
        r1 = lf - hi.astype(F32)
        mid = _bf(r1)
        lo = _bf(r1 - mid.astype(F32))
        cum = _dot(tri, jnp.concatenate([hi, mid, lo], axis=1))
        gs.append(cum[:, :HEAD] + cum[:, HEAD:2 * HEAD] + cum[:, 2 * HEAD:])

    attn = [None] * n_units
    s = CHUNK // 2
    while s >= 1:
        blk = 2 * s
        upper = (row % blk) >= s
        mask = ((r64 // blk) == (c64 // blk)) & ((r64 % blk) >= s) & ((c64 % blk) < s)
        for c in range(n_units):
            g = gs[c]
            if blk >= SUBLANES:
                g_ref = jnp.concatenate(
                    [jnp.broadcast_to(g[b0 + s - 1:b0 + s, :], (blk, HEAD)) for b0 in range(0, CHUNK, blk)],
                    axis=0)
            elif s == 2:
                g_ref = jnp.concatenate(
                    [jnp.where(sub < 4, jnp.broadcast_to(g[v0 + 1:v0 + 2, :], (SUBLANES, HEAD)),
                               jnp.broadcast_to(g[v0 + 5:v0 + 6, :], (SUBLANES, HEAD)))
                     for v0 in range(0, CHUNK, SUBLANES)], axis=0)
            else:
                g_ref = jnp.where(upper, pltpu.roll(g, 1, 0), g)
            z = _bf(jnp.where(upper, qs[c], ks[c]) * jnp.exp(-jnp.abs(g - g_ref)))
            m = _dot_nt(z, z)
            attn[c] = jnp.where(mask, m, 0.0) if attn[c] is None else jnp.where(mask, m, attn[c])
        s //= 2
    eye = r64 == c64
    attn = [jnp.where(eye, jnp.sum(qs[c] * ks[c], axis=-1, keepdims=True), attn[c]) for c in range(n_units)]

    o_intra = [_dot(_bf(attn[c]), vbs[c]) for c in range(n_units)]
    q_dec = [_bf(qs[c] * jnp.exp(gs[c])) for c in range(n_units)]
    g_last = [gs[c][CHUNK - 1:CHUNK, :] for c in range(n_units)]
    kv = [_dot_tn(vbs[c], _bf(ks[c] * jnp.exp(g_last[c] - gs[c]))) for c in range(n_units)]

    states = [st_ref[hd] for hd in range(HGRN_KH)]
    outs = []
    for c, (hd, _, _) in enumerate(units):
        outs.append(_dot_nt(q_dec[c], _bf(states[hd])) + o_intra[c])
        states[hd] = states[hd] * jnp.exp(g_last[c]) + kv[c]
    for hd in range(HGRN_KH):
        st_ref[hd] = states[hd]

    for c, (_, rows, cols) in enumerate(units):
        o = outs[c]
        ms = jnp.mean(o * o, axis=-1, keepdims=True)
        o = o * lax.rsqrt(ms + RMS_EPS) * nw_ref[...]
        o_ref[rows, cols] = _bf(o * _silu(gate_ref[rows, cols].astype(F32)))


def _hgrn(proj, log_f, k, norm_w, tb):
    bsz, s, _ = proj.shape
    width = HGRN_KH * HEAD
    groups = D_MODEL // width
    base = 5 * groups

    def col_spec(off):
        return pl.BlockSpec((None, tb, width), lambda b, h, t: (b, t, base + off * groups + h))

    head_spec = pl.BlockSpec((None, tb, width), lambda b, h, t: (b, t, h))
    return pl.pallas_call(
        functools.partial(_hgrn_kernel, tb=tb),
        grid=(bsz, groups, s // tb),
        in_specs=[col_spec(0), head_spec, head_spec, col_spec(1), col_spec(2),
                  pl.BlockSpec((1, HEAD), lambda b, h, t: (0, 0))],
        out_specs=head_spec,
        out_shape=jax.ShapeDtypeStruct((bsz, s, D_MODEL), BF16),
        scratch_shapes=[pltpu.VMEM((HGRN_KH, HEAD, HEAD), F32)],
        compiler_params=_params("parallel", "parallel", "arbitrary"),
        name="hgrn2",
    )(proj, log_f, k, proj, proj, norm_w.reshape(1, HEAD))


def _unit_lower_inverses(lows):
    r = lax.broadcasted_iota(jnp.int32, (CHUNK, CHUNK), 0)
    c = lax.broadcasted_iota(jnp.int32, (CHUNK, CHUNK), 1)
    eye = jnp.where(r == c, 1.0, 0.0).astype(F32)
    s = 1
    invs = None
    while s < CHUNK:
        sel = ((r // (2 * s)) == (c // (2 * s))) & ((r % (2 * s)) >= s) & ((c % (2 * s)) < s)
        low_s = [jnp.where(sel, low, 0.0) for low in lows]
        if s == 1:
            invs = [eye - x for x in low_s]
        else:
            inv_b = [_bf(x) for x in invs]
            tmp = [_bf(_dot(a, _bf(x))) for a, x in zip(inv_b, low_s)]
            invs = [x - _dot(t, a) for x, t, a in zip(invs, tmp, inv_b)]
        s *= 2
    return invs


def _gdn_kernel(q_ref, k_ref, v_ref, z_ref, gates_ref, avec_ref, dtb_ref, nw_ref, o_ref, st_ref, *, tb):
    tt = pl.program_id(2)
    group = pl.program_id(1)
    n_ch = tb // CHUNK

    @pl.when(tt == 0)
    def _():
        st_ref[...] = jnp.zeros_like(st_ref)

    q_all = [q_ref[:, kk * HEAD:(kk + 1) * HEAD].astype(F32) for kk in range(GDN_KH)]
    k_all = [k_ref[:, kk * HEAD:(kk + 1) * HEAD].astype(F32) for kk in range(GDN_KH)]

    gl = gates_ref[...]
    beta_all = jax.nn.sigmoid(gl)
    g_all = avec_ref[...] * _softplus(gl + dtb_ref[...])
    lane = lax.broadcasted_iota(jnp.int32, (tb, LANES), 1)

    r64 = lax.broadcasted_iota(jnp.int32, (CHUNK, CHUNK), 0)
    c64 = lax.broadcasted_iota(jnp.int32, (CHUNK, CHUNK), 1)
    tri = _bf(jnp.where(r64 >= c64, 1.0, 0.0))
    causal = r64 >= c64
    strict = r64 > c64
    strict_f = jnp.where(strict, 1.0, 0.0).astype(F32)

    chunk_rows = [slice(c * CHUNK, (c + 1) * CHUNK) for c in range(n_ch)]
    n_vh = 2 * GDN_KH
    units = [(vh, c) for c in range(n_ch) for vh in range(n_vh)]

    beta_cols, g_cols = [], []
    for vh in range(n_vh):
        head = n_vh * group + vh
        beta_cols.append(jnp.sum(jnp.where(lane == head, beta_all, 0.0), axis=-1, keepdims=True))
        g_cols.append(jnp.sum(jnp.where(lane == N_V_HEADS + head, g_all, 0.0), axis=-1, keepdims=True))

    kq = [[_dot_nt(_bf(jnp.concatenate([k_all[kk][rows, :], q_all[kk][rows, :]], axis=0)),
                   _bf(k_all[kk][rows, :])) for rows in chunk_rows]
          for kk in range(GDN_KH)]

    decay, g_i, g_last = [], [], []
    for vh, c in units:
        gb = jnp.broadcast_to(g_cols[vh][chunk_rows[c], :], (CHUNK, CHUNK))
        wmat = jnp.concatenate([gb * strict_f, gb], axis=1)
        hi = _bf(wmat)
        lo = _bf(wmat - hi.astype(F32))
        cum = _dot(tri, hi) + _dot(tri, lo)
        decay.append(jnp.where(causal, jnp.exp(jnp.minimum(cum[:, :CHUNK], 0.0)), 0.0))
        g_i.append(cum[:, CHUNK:CHUNK + 1])
        g_last.append(cum[CHUNK - 1:CHUNK, CHUNK:CHUNK + 1])

    lows = [jnp.where(strict, kq[vh // 2][c][:CHUNK, :] * beta_cols[vh][chunk_rows[c], :] * decay[n], 0.0)
            for n, (vh, c) in enumerate(units)]
    invs = _unit_lower_inverses(lows)

    sols = []
    for n, (vh, c) in enumerate(units):
        rows = chunk_rows[c]
        beta = beta_cols[vh][rows, :]
        v = v_ref[rows, vh * HEAD:(vh + 1) * HEAD].astype(F32)
        rhs = jnp.concatenate([v * beta, k_all[vh // 2][rows, :] * (beta * jnp.exp(g_i[n]))], axis=1)
        sols.append(_bf(_dot(_bf(invs[n]), _bf(rhs))))

    o_loc, q_eff, c_mat, p_mat = [], [], [], []
    for n, (vh, c) in enumerate(units):
        rows = chunk_rows[c]
        attn = _bf(kq[vh // 2][c][CHUNK:, :] * decay[n])
        auw = _dot(attn, sols[n])
        o_loc.append(auw[:, :HEAD])
        q_eff.append(_bf(q_all[vh // 2][rows, :] * jnp.exp(g_i[n]) - auw[:, HEAD:]))
        k_tail = _bf(k_all[vh // 2][rows, :] * jnp.exp(g_last[n] - g_i[n]))
        ktuw = _dot_tn(k_tail, sols[n])
        c_mat.append(ktuw[:, :HEAD])
        p_mat.append(_bf(ktuw[:, HEAD:]))

    states = [st_ref[vh] for vh in range(n_vh)]
    outs = [None] * len(units)
    for n, (vh, c) in enumerate(units):
        sb = _bf(states[vh])
        outs[n] = _dot(q_eff[n], sb) + o_loc[n]
        states[vh] = states[vh] * jnp.exp(g_last[n]) + c_mat[n] - _dot(p_mat[n], sb)
    for vh in range(n_vh):
        st_ref[vh] = states[vh]

    for n, (vh, c) in enumerate(units):
        rows = chunk_rows[c]
        o = outs[n]
        ms = jnp.mean(o * o, axis=-1, keepdims=True)
        o = o * lax.rsqrt(ms + RMS_EPS) * nw_ref[...]
        z = z_ref[rows, vh * HEAD:(vh + 1) * HEAD].astype(F32)
        o_ref[rows, vh * HEAD:(vh + 1) * HEAD] = _bf(o * _silu(z))


def _gdn(qkv, z, gate_logits, a_vec, dtb_vec, norm_w, tb):
    bsz, s, _ = qkv.shape
    width = GDN_KH * HEAD
    groups = D_MODEL // width

    in_specs = [
        pl.BlockSpec((None, tb, width), lambda b, h, t: (b, t, h)),
        pl.BlockSpec((None, tb, width), lambda b, h, t: (b, t, groups + h)),
        pl.BlockSpec((None, tb, 2 * width), lambda b, h, t: (b, t, groups + h)),
        pl.BlockSpec((None, tb, 2 * width), lambda b, h, t: (b, t, h)),
        pl.BlockSpec((None, tb, LANES), lambda b, h, t: (b, t, 0)),
        pl.BlockSpec((1, LANES), lambda b, h, t: (0, 0)),
        pl.BlockSpec((1, LANES), lambda b, h, t: (0, 0)),
        pl.BlockSpec((1, HEAD), lambda b, h, t: (0, 0)),
    ]
    return pl.pallas_call(
        functools.partial(_gdn_kernel, tb=tb),
        grid=(bsz, groups, s // tb),
        in_specs=in_specs,
        out_specs=pl.BlockSpec((None, tb, 2 * width), lambda b, h, t: (b, t, h)),
        out_shape=jax.ShapeDtypeStruct((bsz, s, 2 * D_MODEL), BF16),
        scratch_shapes=[pltpu.VMEM((2 * GDN_KH, HEAD, HEAD), F32)],
        compiler_params=_params("parallel", "parallel", "arbitrary"),
        name="gated_deltanet",
    )(qkv, qkv, qkv, z, gate_logits, a_vec, dtb_vec, norm_w.reshape(1, HEAD))


def kernel(x, norm_w, final_norm_w, even_w_in, even_w_out, hgrn_lb_logits, hgrn_norm_w,
           odd_w_in, odd_conv_w, odd_dt_bias, odd_a_log, odd_norm_w, odd_w_out):
    bsz, s, d = x.shape
    depth = norm_w.shape[0]
    t = bsz * s
    tm = 1024
    tm_in = 1024
    tb = 512

    lb_all = jnp.cumsum(jax.nn.softmax(hgrn_lb_logits.astype(F32), axis=0), axis=0)
    lb_all = jnp.maximum(lb_all - lb_all[0:1], 0.0)
    log_lb = jnp.log(lb_all)
    log_1mlb = jnp.log1p(-lb_all)

    h = x.reshape(t, d)
    for layer in range(depth):
        j = layer // 2
        final_w = final_norm_w if layer == depth - 1 else None
        if layer % 2 == 0:
            w = even_w_in[j]
            w = _bf(jnp.concatenate([w[:, 5 * d:6 * d], w[:, :5 * d], w[:, 6 * d:]], axis=1))
            proj, log_f, k = _even_proj(h, norm_w[layer], w, log_lb[j], log_1mlb[j], tm_in)
            proj = proj.reshape(bsz, s, EVEN_IN)
            a_mix = _dilated_attention(proj)
            b_mix = _hgrn(proj, log_f.reshape(bsz, s, d), k.reshape(bsz, s, d), hgrn_norm_w[j], 2 * tb)
            w_out = _bf(even_w_out[j])
            h = _out_proj(h, [a_mix.reshape(t, d), b_mix.reshape(t, d)], [w_out[:d], w_out[d:]],
                          final_w, tm)
        else:
            n_rest = odd_w_in.shape[2] - ODD_CONV
            pad = jnp.zeros((d, ODD_REST_PAD - n_rest), F32)
            w_qkv = _bf(odd_w_in[j][:, :ODD_CONV])
            w_rest = _bf(jnp.concatenate([odd_w_in[j][:, ODD_CONV:], pad], axis=1))
            qkv = _qkv_conv_proj(h, norm_w[layer], w_qkv, odd_conv_w[j], tm_in, s)
            z, gate_logits = _norm_matmul(h, norm_w[layer], w_rest, tm_in, ODD_REST_PAD,
                                          side_start=ODD_REST_PAD - LANES, side_width=LANES)
            zeros16 = jnp.zeros((N_V_HEADS,), F32)
            tail = jnp.zeros((LANES - 2 * N_V_HEADS,), F32)
            a_vec = jnp.concatenate([zeros16, -jnp.exp(odd_a_log[j].astype(F32)), tail]).reshape(1, LANES)
            dtb_vec = jnp.concatenate([zeros16, odd_dt_bias[j].astype(F32), tail]).reshape(1, LANES)
            o = _gdn(qkv.reshape(bsz, s, ODD_CONV), z.reshape(bsz, s, ODD_REST_PAD),
                     gate_logits.reshape(bsz, s, LANES), a_vec, dtb_vec, odd_norm_w[j], tb)
            h = _out_proj(h, [o.reshape(t, 2 * d)], [_bf(odd_w_out[j])], final_w, tm)
    return h.reshape(bsz, s, d)
```

```python
import functools

import jax
import jax.numpy as jnp
from jax import lax
from jax.experimental import pallas as pl
from jax.experimental.pallas import tpu as pltpu

F32 = jnp.float32
BF16 = jnp.bfloat16

RMS_EPS = 1e-6
LANES = 128
SUBLANES = 8
VMEM_LIMIT = 48 * 1024 * 1024

D_MODEL = 1024
A_HEAD_DIM = 64
A_BLOCK = 128
A_DILATIONS = (1, 4, 16)
CHUNK = 64
A_SCALE = A_HEAD_DIM ** -0.5 * 1.4426950408889634
A_MOD = 16
A_GROUP = 8
HEAD = 128
CONV_K = 4
HGRN_KH = 2
GDN_KH = 2

EVEN_IN = 8 * D_MODEL
ODD_CONV = 4 * D_MODEL
QK_TILES = 2
ODD_REST_PAD = 2 * D_MODEL + LANES
N_V_HEADS = 16


def _dot(a, b):
    return jnp.dot(a, b, preferred_element_type=F32)


def _dot_nt(a, b, precision=None):
    return lax.dot_general(a, b, (((1,), (1,)), ((), ())), precision=precision,
                           preferred_element_type=F32)


def _dot_tn(a, b):
    return lax.dot_general(a, b, (((0,), (0,)), ((), ())), preferred_element_type=F32)


def _bf(x):
    return x.astype(BF16)


def _silu(x):
    return x * jax.nn.sigmoid(x)


def _softplus(x):
    return jnp.maximum(x, 0.0) + jnp.log1p(jnp.exp(-jnp.abs(x)))


def _params(*sem):
    return pltpu.CompilerParams(dimension_semantics=sem, vmem_limit_bytes=VMEM_LIMIT)


def _norm_rows(x_ref, nw_ref, xn_ref):
    x = x_ref[...]
    ms = jnp.mean(x * x, axis=-1, keepdims=True)
    xn_ref[...] = _bf(x * lax.rsqrt(ms + RMS_EPS) * nw_ref[...])


def _norm_matmul_kernel(x_ref, nw_ref, w_ref, o_ref, side_ref, xn_ref, *, side_tile, side_lo):
    j = pl.program_id(1)
    pl.when(j == 0)(functools.partial(_norm_rows, x_ref, nw_ref, xn_ref))
    acc = _dot(xn_ref[...], w_ref[...])
    o_ref[...] = _bf(acc)

    @pl.when(j == side_tile)
    def _():
        side_ref[...] = acc[:, side_lo:side_lo + side_ref.shape[1]]


def _norm_matmul(x, nw, w, tm, tn, side_start, side_width):
    t, d = x.shape
    n = w.shape[1]
    side_tile, side_lo = divmod(side_start, tn)
    assert side_lo + side_width <= tn
    return pl.pallas_call(
        functools.partial(_norm_matmul_kernel, side_tile=side_tile, side_lo=side_lo),
        grid=(t // tm, n // tn),
        in_specs=[pl.BlockSpec((tm, d), lambda i, j: (i, 0)),
                  pl.BlockSpec((1, d), lambda i, j: (0, 0)),
                  pl.BlockSpec((d, tn), lambda i, j: (0, j))],
        out_specs=[pl.BlockSpec((tm, tn), lambda i, j: (i, j)),
                   pl.BlockSpec((tm, side_width), lambda i, j: (i, 0))],
        out_shape=[jax.ShapeDtypeStruct((t, n), BF16),
                   jax.ShapeDtypeStruct((t, side_width), F32)],
        scratch_shapes=[pltpu.VMEM((tm, d), BF16)],
        compiler_params=_params("parallel", "arbitrary"),
        name="norm_in_proj",
    )(x, nw.reshape(1, d), w)


def _even_proj_kernel(x_ref, nw_ref, w_ref, loglb_ref, log1mlb_ref, o_ref, logf_ref, k_ref, xn_ref,
                      fbuf, *, n_tiles):
    j = pl.program_id(1)
    pl.when(j == 0)(functools.partial(_norm_rows, x_ref, nw_ref, xn_ref))
    acc = _dot(xn_ref[...], w_ref[...])
    o_ref[...] = _bf(acc)

    @pl.when(j == 0)
    def _():
        fbuf[...] = acc

    step_rows = fbuf.shape[0] // n_tiles
    rows = pl.ds(pl.multiple_of(j * step_rows, step_rows), step_rows)
    x = fbuf[rows, :]
    log_sig = jnp.minimum(x, 0.0) - jnp.log1p(jnp.exp(-jnp.abs(x)))
    log_lb = loglb_ref[...]
    b = log1mlb_ref[...] + log_sig
    logf_ref[rows, :] = jnp.maximum(log_lb, b) + jnp.log1p(jnp.exp(-jnp.abs(log_lb - b)))
    k_ref[rows, :] = _bf(jnp.exp(b - x))


def _even_proj(x, nw, w, log_lb, log_1mlb, tm):
    t, d = x.shape
    n = w.shape[1]
    n_tiles = n // d
    f_tile = 5

    def w_tile(i, j):
        return 0, jnp.where(j == 0, f_tile, jnp.where(j <= f_tile, j - 1, j))

    return pl.pallas_call(
        functools.partial(_even_proj_kernel, n_tiles=n_tiles),
        grid=(t // tm, n_tiles),
        in_specs=[pl.BlockSpec((tm, d), lambda i, j: (i, 0)),
                  pl.BlockSpec((1, d), lambda i, j: (0, 0)),
                  pl.BlockSpec((d, d), w_tile),
                  pl.BlockSpec((1, d), lambda i, j: (0, 0)),
                  pl.BlockSpec((1, d), lambda i, j: (0, 0))],
        out_specs=[pl.BlockSpec((tm, d), lambda i, j: (i, j)),
                   pl.BlockSpec((tm, d), lambda i, j: (i, 0)),
                   pl.BlockSpec((tm, d), lambda i, j: (i, 0))],
        out_shape=[jax.ShapeDtypeStruct((t, n), BF16),
                   jax.ShapeDtypeStruct((t, d), F32),
                   jax.ShapeDtypeStruct((t, d), BF16)],
        scratch_shapes=[pltpu.VMEM((tm, d), BF16), pltpu.VMEM((tm, d), F32)],
        compiler_params=_params("parallel", "arbitrary"),
        name="even_in_proj",
    )(x, nw.reshape(1, d), w, log_lb.reshape(1, d), log_1mlb.reshape(1, d))


def _qkv_conv_kernel(x_ref, nw_ref, w_ref, cw_ref, o_ref, xn_ref, halo_ref, *, tiles_per_seq):
    i = pl.program_id(0)
    j = pl.program_id(1)
    tm, tn = o_ref.shape
    halo = halo_ref.shape[1]

    pl.when(j == 0)(functools.partial(_norm_rows, x_ref, nw_ref, xn_ref))

    @pl.when(i == 0)
    def _():
        halo_ref[j] = jnp.zeros((halo, tn), F32)

    acc = _dot(xn_ref[...], w_ref[...])
    prev = jnp.where(i % tiles_per_seq == 0, 0.0, halo_ref[j])
    halo_ref[j] = acc[tm - halo:, :]
    ext = jnp.concatenate([prev, acc], axis=0)
    cw = cw_ref[...]
    conv = ext * cw[0:1, :]
    for k in range(1, CONV_K):
        conv = ext * cw[k:k + 1, :] + pltpu.roll(conv, 1, 0)
    y = _silu(conv[halo:, :])

    @pl.when(j < QK_TILES)
    def _():
        scale = jnp.where(j == 0, HEAD ** -0.5, 1.0)
        for h0 in range(0, tn, HEAD):
            yh = y[:, h0:h0 + HEAD]
            inv = lax.rsqrt(jnp.sum(yh * yh, axis=-1, keepdims=True) + RMS_EPS) * scale
            o_ref[:, h0:h0 + HEAD] = _bf(yh * inv)

    @pl.when(j >= QK_TILES)
    def _():
        o_ref[...] = _bf(y)


def _qkv_conv_proj(x, nw, w, conv_w, tm, seq):
    t, d = x.shape
    n = conv_w.shape[1]
    tn = d
    return pl.pallas_call(
        functools.partial(_qkv_conv_kernel, tiles_per_seq=seq // tm),
        grid=(t // tm, n // tn),
        in_specs=[pl.BlockSpec((tm, d), lambda i, j: (i, 0)),
                  pl.BlockSpec((1, d), lambda i, j: (0, 0)),
                  pl.BlockSpec((d, tn), lambda i, j: (0, j)),
                  pl.BlockSpec((CONV_K, tn), lambda i, j: (0, j))],
        out_specs=pl.BlockSpec((tm, tn), lambda i, j: (i, j)),
        out_shape=jax.ShapeDtypeStruct((t, n), BF16),
        scratch_shapes=[pltpu.VMEM((tm, d), BF16),
                        pltpu.VMEM((n // tn, SUBLANES, tn), F32)],
        compiler_params=_params("arbitrary", "arbitrary"),
        name="qkv_conv_proj",
    )(x, nw.reshape(1, d), w, conv_w)


def _out_proj_kernel(*refs, n_in, final):
    h_ref = refs[0]
    a_refs = refs[1:1 + n_in]
    w_refs = refs[1 + n_in:1 + 2 * n_in]
    o_ref = refs[-1]
    acc = h_ref[...]
    for a_ref, w_ref in zip(a_refs, w_refs):
        acc = acc + _dot(a_ref[...], w_ref[...])
    if final:
        fw_ref = refs[1 + 2 * n_in]
        ms = jnp.mean(acc * acc, axis=-1, keepdims=True)
        acc = acc * lax.rsqrt(ms + RMS_EPS) * fw_ref[...]
    o_ref[...] = acc


def _out_proj(h, acts, w, final_w, tm):
    t, d = h.shape
    n_in = len(acts)
    k = acts[0].shape[1]
    final = final_w is not None
    in_specs = [pl.BlockSpec((tm, d), lambda i: (i, 0))]
    in_specs += [pl.BlockSpec((tm, k), lambda i: (i, 0)) for _ in acts]
    in_specs += [pl.BlockSpec((k, d), lambda i, rb=rb: (rb, 0)) for rb in range(n_in)]
    args = [h, *acts] + [w] * n_in
    if final:
        in_specs.append(pl.BlockSpec((1, d), lambda i: (0, 0)))
        args.append(final_w.reshape(1, d))
    return pl.pallas_call(
        functools.partial(_out_proj_kernel, n_in=n_in, final=final),
        grid=(t // tm,),
        in_specs=in_specs,
        out_specs=pl.BlockSpec((tm, d), lambda i: (i, 0)),
        out_shape=jax.ShapeDtypeStruct((t, d), F32),
        compiler_params=_params("parallel"),
        name="out_proj",
    )(*args)


def _attn_kernel(q_ref, k_ref, v_ref, gate_ref, o_ref, qp_ref, kp_ref, vp_ref, acc_ref, m_ref, l_ref,
                 bias_ref, *, seq):
    per_res = seq // A_MOD
    n_blocks = seq // A_BLOCK
    row = lax.broadcasted_iota(jnp.int32, (A_BLOCK, 2 * A_BLOCK), 0)
    col = lax.broadcasted_iota(jnp.int32, (A_BLOCK, 2 * A_BLOCK), 1)
    lane = lax.broadcasted_iota(jnp.int32, (A_BLOCK, LANES), 1)
    head0 = lane < A_HEAD_DIM

    def widen(t, carry):
        rows = pl.ds(pl.multiple_of(t * A_BLOCK, A_BLOCK), A_BLOCK)
        acc_ref[rows, :] = q_ref[rows, :].astype(F32) * A_SCALE
        m_ref[rows, :] = k_ref[rows, :].astype(F32)
        l_ref[rows, :] = v_ref[rows, :].astype(F32)
        return carry

    lax.fori_loop(0, n_blocks, widen, 0)

    def to_residue_major(res, carry):
        src = pl.ds(res, per_res, stride=A_MOD)
        dst = pl.ds(pl.multiple_of(res * per_res, per_res), per_res)
        qp_ref[dst, :] = acc_ref[src, :]
        kp_ref[dst, :] = m_ref[src, :]
        vp_ref[dst, :] = l_ref[src, :]
        return carry

    lax.fori_loop(0, A_MOD, to_residue_major, 0)

    for dilation in A_DILATIONS:
        first = dilation == A_DILATIONS[0]
        per_class = n_blocks // dilation
        runs = A_MOD // dilation
        rq = A_BLOCK // runs
        dist = runs * (row % rq - col % (2 * rq)) + (row // rq - col // (2 * rq)) + A_BLOCK
        band = (dist >= 0) & (dist <= A_BLOCK)
        in_cur = col % (2 * rq) >= rq
        slot = 2 * A_DILATIONS.index(dilation)
        bias_ref[slot] = jnp.where(band & in_cur, 0.0, -jnp.inf)
        bias_ref[slot + 1] = jnp.where(band, 0.0, -jnp.inf)

        def load(n, dilation=dilation, first=first, per_class=per_class, runs=runs, rq=rq):
            res = n // per_class
            i = n % per_class
            cur, prev = [], []
            for c in range(runs):
                base = pl.multiple_of((c * dilation + res) * per_res + rq * i, SUBLANES)
                cur.append(pl.ds(base, rq))
                prev.append(pl.ds(pl.multiple_of(jnp.where(i > 0, base - rq, base), SUBLANES), rq))

            def window(ref):
                return _bf(jnp.concatenate([ref[rows, :] for pc in zip(prev, cur) for rows in pc], axis=0))

            def block(ref):
                return jnp.concatenate([ref[rows, :] for rows in cur], axis=0)

            return cur, i, block(qp_ref), window(kp_ref), window(vp_ref)

        def scores(i, q, kw, vw, slot=slot):
            bias = bias_ref[slot + jnp.where(i > 0, 1, 0)]
            parts = []
            for h in range(2):
                qh = _bf(jnp.where(head0 if h == 0 else ~head0, q, 0.0))
                s = _dot_nt(qh, kw) + bias
                m_blk = jnp.max(s, axis=-1, keepdims=True)
                p = jnp.exp2(s - m_blk)
                parts.append((_dot(_bf(p), vw), m_blk, jnp.sum(p, axis=-1, keepdims=True)))
            return tuple(jnp.where(head0, a, b) for a, b in zip(*parts))

        def group(t, carry, load=load, scores=scores, first=first, rq=rq):
            loaded = [load(t * A_GROUP + u) for u in range(A_GROUP)]
            results = [scores(*item[1:]) for item in loaded]
            for (cur, *_), (acc, m_blk, l_blk) in zip(loaded, results):
                if not first:
                    acc_old, m_old, l_old = (
                        jnp.concatenate([ref[rows, :] for rows in cur], axis=0)
                        for ref in (acc_ref, m_ref, l_ref))
                    m_new = jnp.maximum(m_old, m_blk)
                    alpha = jnp.exp2(m_old - m_new)
                    beta = jnp.exp2(m_blk - m_new)
                    acc = acc_old * alpha + acc * beta
                    l_blk = l_old * alpha + l_blk * beta
                    m_blk = m_new
                for ref, val in zip((acc_ref, m_ref, l_ref), (acc, m_blk, l_blk)):
                    for c, rows in enumerate(cur):
                        ref[rows, :] = val[c * rq:(c + 1) * rq, :]
            return carry

        lax.fori_loop(0, n_blocks // A_GROUP, group, 0)

    def to_position_order(res, carry):
        src = pl.ds(pl.multiple_of(res * per_res, per_res), per_res)
        m_ref[pl.ds(res, per_res, stride=A_MOD), :] = acc_ref[src, :] / l_ref[src, :]
        return carry

    lax.fori_loop(0, A_MOD, to_position_order, 0)

    def finish(t, carry):
        rows = pl.ds(pl.multiple_of(t * A_BLOCK, A_BLOCK), A_BLOCK)
        o_ref[rows, :] = _bf(m_ref[rows, :] * _silu(gate_ref[rows, :].astype(F32)))
        return carry

    lax.fori_loop(0, n_blocks, finish, 0)


def _dilated_attention(proj):
    bsz, s, _ = proj.shape
    hp_blocks = D_MODEL // LANES

    def col_spec(off):
        return pl.BlockSpec((None, s, LANES), lambda b, hp: (b, 0, (off + 1) * hp_blocks + hp))

    return pl.pallas_call(
        functools.partial(_attn_kernel, seq=s),
        grid=(bsz, hp_blocks),
        in_specs=[col_spec(0), col_spec(1), col_spec(2), col_spec(3)],
        out_specs=pl.BlockSpec((None, s, LANES), lambda b, hp: (b, 0, hp)),
        out_shape=jax.ShapeDtypeStruct((bsz, s, D_MODEL), BF16),
        scratch_shapes=[pltpu.VMEM((s, LANES), F32)] * 6
        + [pltpu.VMEM((2 * len(A_DILATIONS), A_BLOCK, 2 * A_BLOCK), F32)],
        compiler_params=_params("parallel", "parallel"),
        name="dilated_attn",
    )(proj, proj, proj, proj)


def _hgrn_kernel(q_ref, logf_ref, k_ref, i_ref, gate_ref, nw_ref, o_ref, st_ref, *, tb):
    @pl.when(pl.program_id(2) == 0)
    def _():
        st_ref[...] = jnp.zeros_like(st_ref)

    n_ch = tb // CHUNK
    r64 = lax.broadcasted_iota(jnp.int32, (CHUNK, CHUNK), 0)
    c64 = lax.broadcasted_iota(jnp.int32, (CHUNK, CHUNK), 1)
    tri = _bf(jnp.where(r64 >= c64, 1.0, 0.0))
    row = lax.broadcasted_iota(jnp.int32, (CHUNK, HEAD), 0)
    sub = lax.broadcasted_iota(jnp.int32, (SUBLANES, HEAD), 0)
    units = [(hd, slice(c * CHUNK, (c + 1) * CHUNK), slice(hd * HEAD, (hd + 1) * HEAD))
             for c in range(n_ch) for hd in range(HGRN_KH)]
    n_units = len(units)
    qs = [q_ref[rows, cols].astype(F32) for _, rows, cols in units]
    ks = [k_ref[rows, cols].astype(F32) for _, rows, cols in units]
    vbs = [i_ref[rows, cols] for _, rows, cols in units]

    gs = []
    for _, rows, cols in units:
        lf = logf_ref[rows, cols]
        hi = _bf(lf)
        r1 = lf - hi.astype(F32)
        mid = _bf(r1)
        lo = _bf(r1 - mid.astype(F32))
        cum = _dot(tri, jnp.concatenate([hi, mid, lo], axis=1))
        gs.append(cum[:, :HEAD] + cum[:, HEAD:2 * HEAD] + cum[:, 2 * HEAD:])

    attn = [None] * n_units
    s = CHUNK // 2
    while s >= 1:
        blk = 2 * s
        upper = (row % blk) >= s
        mask = ((r64 // blk) == (c64 // blk)) & ((r64 % blk) >= s) & ((c64 % blk) < s)
        for c in range(n_units):
            g = gs[c]
            if blk >= SUBLANES:
                g_ref = jnp.concatenate(
                    [jnp.broadcast_to(g[b0 + s - 1:b0 + s, :], (blk, HEAD)) for b0 in range(0, CHUNK, blk)],
                    axis=0)
            elif s == 2:
                g_ref = jnp.concatenate(
                    [jnp.where(sub < 4, jnp.broadcast_to(g[v0 + 1:v0 + 2, :], (SUBLANES, HEAD)),
                               jnp.broadcast_to(g[v0 + 5:v0 + 6, :], (SUBLANES, HEAD)))
                     for v0 in range(0, CHUNK, SUBLANES)], axis=0)
            else:
                g_ref = jnp.where(upper, pltpu.roll(g, 1, 0), g)
            z = _bf(jnp.where(upper, qs[c], ks[c]) * jnp.exp(-jnp.abs(g - g_ref)))
            m = _dot_nt(z, z)
            attn[c] = jnp.where(mask, m, 0.0) if attn[c] is None else jnp.where(mask, m, attn[c])
        s //= 2
    eye = r64 == c64
    attn = [jnp.where(eye, jnp.sum(qs[c] * ks[c], axis=-1, keepdims=True), attn[c]) for c in range(n_units)]

    o_intra = [_dot(_bf(attn[c]), vbs[c]) for c in range(n_units)]
    q_dec = [_bf(qs[c] * jnp.exp(gs[c])) for c in range(n_units)]
    g_last = [gs[c][CHUNK - 1:CHUNK, :] for c in range(n_units)]
    kv = [_dot_tn(vbs[c], _bf(ks[c] * jnp.exp(g_last[c] - gs[c]))) for c in range(n_units)]

    states = [st_ref[hd] for hd in range(HGRN_KH)]
    outs = []
    for c, (hd, _, _) in enumerate(units):
        outs.append(_dot_nt(q_dec[c], _bf(states[hd])) + o_intra[c])
        states[hd] = states[hd] * jnp.exp(g_last[c]) + kv[c]
    for hd in range(HGRN_KH):
        st_ref[hd] = states[hd]

    for c, (_, rows, cols) in enumerate(units):
        o = outs[c]
        ms = jnp.mean(o * o, axis=-1, keepdims=True)
        o = o * lax.rsqrt(ms + RMS_EPS) * nw_ref[...]
        o_ref[rows, cols] = _bf(o * _silu(gate_ref[rows, cols].astype(F32)))


def _hgrn(proj, log_f, k, norm_w, tb):
    bsz, s, _ = proj.shape
    width = HGRN_KH * HEAD
    groups = D_MODEL // width
    base = 5 * groups

    def col_spec(off):
        return pl.BlockSpec((None, tb, width), lambda b, h, t: (b, t, base + off * groups + h))

    head_spec = pl.BlockSpec((None, tb, width), lambda b, h, t: (b, t, h))
    return pl.pallas_call(
        functools.partial(_hgrn_kernel, tb=tb),
        grid=(bsz, groups, s // tb),
        in_specs=[col_spec(0), head_spec, head_spec, col_spec(1), col_spec(2),
                  pl.BlockSpec((1, HEAD), lambda b, h, t: (0, 0))],
        out_specs=head_spec,
        out_shape=jax.ShapeDtypeStruct((bsz, s, D_MODEL), BF16),
        scratch_shapes=[pltpu.VMEM((HGRN_KH, HEAD, HEAD), F32)],
        compiler_params=_params("parallel", "parallel", "arbitrary"),
        name="hgrn2",
    )(proj, log_f, k, proj, proj, norm_w.reshape(1, HEAD))


def _unit_lower_inverses(lows):
    r = lax.broadcasted_iota(jnp.int32, (CHUNK, CHUNK), 0)
    c = lax.broadcasted_iota(jnp.int32, (CHUNK, CHUNK), 1)
    eye = jnp.where(r == c, 1.0, 0.0).astype(F32)
    s = 1
    invs = None
    while s < CHUNK:
        sel = ((r // (2 * s)) == (c // (2 * s))) & ((r % (2 * s)) >= s) & ((c % (2 * s)) < s)
        low_s = [jnp.where(sel, low, 0.0) for low in lows]
        if s == 1:
            invs = [eye - x for x in low_s]
        else:
            inv_b = [_bf(x) for x in invs]
            tmp = [_bf(_dot(a, _bf(x))) for a, x in zip(inv_b, low_s)]
            invs = [x - _dot(t, a) for x, t, a in zip(invs, tmp, inv_b)]
        s *= 2
    return invs


def _gdn_kernel(q_ref, k_ref, v_ref, z_ref, gates_ref, avec_ref, dtb_ref, nw_ref, o_ref, st_ref, *, tb):
    tt = pl.program_id(2)
    group = pl.program_id(1)
    n_ch = tb // CHUNK

    @pl.when(tt == 0)
    def _():
        st_ref[...] = jnp.zeros_like(st_ref)

    q_all = [q_ref[:, kk * HEAD:(kk + 1) * HEAD].astype(F32) for kk in range(GDN_KH)]
    k_all = [k_ref[:, kk * HEAD:(kk + 1) * HEAD].astype(F32) for kk in range(GDN_KH)]

    gl = gates_ref[...]
    beta_all = jax.nn.sigmoid(gl)
    g_all = avec_ref[...] * _softplus(gl + dtb_ref[...])
    lane = lax.broadcasted_iota(jnp.int32, (tb, LANES), 1)

    r64 = lax.broadcasted_iota(jnp.int32, (CHUNK, CHUNK), 0)
    c64 = lax.broadcasted_iota(jnp.int32, (CHUNK, CHUNK), 1)
    tri = _bf(jnp.where(r64 >= c64, 1.0, 0.0))
    causal = r64 >= c64
    strict = r64 > c64
    strict_f = jnp.where(strict, 1.0, 0.0).astype(F32)

    chunk_rows = [slice(c * CHUNK, (c + 1) * CHUNK) for c in range(n_ch)]
    n_vh = 2 * GDN_KH
    units = [(vh, c) for c in range(n_ch) for vh in range(n_vh)]

    beta_cols, g_cols = [], []
    for vh in range(n_vh):
        head = n_vh * group + vh
        beta_cols.append(jnp.sum(jnp.where(lane == head, beta_all, 0.0), axis=-1, keepdims=True))
        g_cols.append(jnp.sum(jnp.where(lane == N_V_HEADS + head, g_all, 0.0), axis=-1, keepdims=True))

    kq = [[_dot_nt(_bf(jnp.concatenate([k_all[kk][rows, :], q_all[kk][rows, :]], axis=0)),
                   _bf(k_all[kk][rows, :])) for rows in chunk_rows]
          for kk in range(GDN_KH)]

    decay, g_i, g_last = [], [], []
    for vh, c in units:
        gb = jnp.broadcast_to(g_cols[vh][chunk_rows[c], :], (CHUNK, CHUNK))
        wmat = jnp.concatenate([gb * strict_f, gb], axis=1)
        hi = _bf(wmat)
        lo = _bf(wmat - hi.astype(F32))
        cum = _dot(tri, hi) + _dot(tri, lo)
        decay.append(jnp.where(causal, jnp.exp(jnp.minimum(cum[:, :CHUNK], 0.0)), 0.0))
        g_i.append(cum[:, CHUNK:CHUNK + 1])
        g_last.append(cum[CHUNK - 1:CHUNK, CHUNK:CHUNK + 1])

    lows = [jnp.where(strict, kq[vh // 2][c][:CHUNK, :] * beta_cols[vh][chunk_rows[c], :] * decay[n], 0.0)
            for n, (vh, c) in enumerate(units)]
    invs = _unit_lower_inverses(lows)

    sols = []
    for n, (vh, c) in enumerate(units):
        rows = chunk_rows[c]
        beta = beta_cols[vh][rows, :]
        v = v_ref[rows, vh * HEAD:(vh + 1) * HEAD].astype(F32)
        rhs = jnp.concatenate([v * beta, k_all[vh // 2][rows, :] * (beta * jnp.exp(g_i[n]))], axis=1)
        sols.append(_bf(_dot(_bf(invs[n]), _bf(rhs))))

    o_loc, q_eff, c_mat, p_mat = [], [], [], []
    for n, (vh, c) in enumerate(units):
        rows = chunk_rows[c]
        attn = _bf(kq[vh // 2][c][CHUNK:, :] * decay[n])
        auw = _dot(attn, sols[n])
        o_loc.append(auw[:, :HEAD])
        q_eff.append(_bf(q_all[vh // 2][rows, :] * jnp.exp(g_i[n]) - auw[:, HEAD:]))
        k_tail = _bf(k_all[vh // 2][rows, :] * jnp.exp(g_last[n] - g_i[n]))
        ktuw = _dot_tn(k_tail, sols[n])
        c_mat.append(ktuw[:, :HEAD])
        p_mat.append(_bf(ktuw[:, HEAD:]))

    states = [st_ref[vh] for vh in range(n_vh)]
    outs = [None] * len(units)
    for n, (vh, c) in enumerate(units):
        sb = _bf(states[vh])
        outs[n] = _dot(q_eff[n], sb) + o_loc[n]
        states[vh] = states[vh] * jnp.exp(g_last[n]) + c_mat[n] - _dot(p_mat[n], sb)
    for vh in range(n_vh):
        st_ref[vh] = states[vh]

    for n, (vh, c) in enumerate(units):
        rows = chunk_rows[c]
        o = outs[n]
        ms = jnp.mean(o * o, axis=-1, keepdims=True)
        o = o * lax.rsqrt(ms + RMS_EPS) * nw_ref[...]
        z = z_ref[rows, vh * HEAD:(vh + 1) * HEAD].astype(F32)
        o_ref[rows, vh * HEAD:(vh + 1) * HEAD] = _bf(o * _silu(z))


def _gdn(qkv, z, gate_logits, a_vec, dtb_vec, norm_w, tb):
    bsz, s, _ = qkv.shape
    width = GDN_KH * HEAD
    groups = D_MODEL // width

    in_specs = [
        pl.BlockSpec((None, tb, width), lambda b, h, t: (b, t, h)),
        pl.BlockSpec((None, tb, width), lambda b, h, t: (b, t, groups + h)),
        pl.BlockSpec((None, tb, 2 * width), lambda b, h, t: (b, t, groups + h)),
        pl.BlockSpec((None, tb, 2 * width), lambda b, h, t: (b, t, h)),
        pl.BlockSpec((None, tb, LANES), lambda b, h, t: (b, t, 0)),
        pl.BlockSpec((1, LANES), lambda b, h, t: (0, 0)),
        pl.BlockSpec((1, LANES), lambda b, h, t: (0, 0)),
        pl.BlockSpec((1, HEAD), lambda b, h, t: (0, 0)),
    ]
    return pl.pallas_call(
        functools.partial(_gdn_kernel, tb=tb),
        grid=(bsz, groups, s // tb),
        in_specs=in_specs,
        out_specs=pl.BlockSpec((None, tb, 2 * width), lambda b, h, t: (b, t, h)),
        out_shape=jax.ShapeDtypeStruct((bsz, s, 2 * D_MODEL), BF16),
        scratch_shapes=[pltpu.VMEM((2 * GDN_KH, HEAD, HEAD), F32)],
        compiler_params=_params("parallel", "parallel", "arbitrary"),
        name="gated_deltanet",
    )(qkv, qkv, qkv, z, gate_logits, a_vec, dtb_vec, norm_w.reshape(1, HEAD))


def kernel(x, norm_w, final_norm_w, even_w_in, even_w_out, hgrn_lb_logits, hgrn_norm_w,
           odd_w_in, odd_conv_w, odd_dt_bias, odd_a_log, odd_norm_w, odd_w_out):
    bsz, s, d = x.shape
    depth = norm_w.shape[0]
    t = bsz * s
    tm = 1024
    tm_in = 1024
    tb = 512

    lb_all = jnp.cumsum(jax.nn.softmax(hgrn_lb_logits.astype(F32), axis=0), axis=0)
    lb_all = jnp.maximum(lb_all - lb_all[0:1], 0.0)
    log_lb = jnp.log(lb_all)
    log_1mlb = jnp.log1p(-lb_all)

    h = x.reshape(t, d)
    for layer in range(depth):
        j = layer // 2
        final_w = final_norm_w if layer == depth - 1 else None
        if layer % 2 == 0:
            proj, log_f, k = _even_proj(h, norm_w[layer], _bf(even_w_in[j]), log_lb[j], log_1mlb[j],
                                        tm_in)
            proj = proj.reshape(bsz, s, EVEN_IN)
            a_mix = _dilated_attention(proj)
            b_mix = _hgrn(proj, log_f.reshape(bsz, s, d), k.reshape(bsz, s, d), hgrn_norm_w[j], 2 * tb)
            h = _out_proj(h, [a_mix.reshape(t, d), b_mix.reshape(t, d)], _bf(even_w_out[j]),
                          final_w, tm)
        else:
            w_in = _bf(odd_w_in[j])
            pad = jnp.zeros((d, ODD_REST_PAD - (w_in.shape[1] - ODD_CONV)), BF16)
            w_rest = jnp.concatenate([w_in[:, ODD_CONV:], pad], axis=1)
            qkv = _qkv_conv_proj(h, norm_w[layer], w_in, odd_conv_w[j], tm_in, s)
            z, gate_logits = _norm_matmul(h, norm_w[layer], w_rest, tm_in, ODD_REST_PAD,
                                          side_start=ODD_REST_PAD - LANES, side_width=LANES)
            zeros16 = jnp.zeros((N_V_HEADS,), F32)
            tail = jnp.zeros((LANES - 2 * N_V_HEADS,), F32)
            a_vec = jnp.concatenate([zeros16, -jnp.exp(odd_a_log[j].astype(F32)), tail]).reshape(1, LANES)
            dtb_vec = jnp.concatenate([zeros16, odd_dt_bias[j].astype(F32), tail]).reshape(1, LANES)
            o = _gdn(qkv.reshape(bsz, s, ODD_CONV), z.reshape(bsz, s, ODD_REST_PAD),
                     gate_logits.reshape(bsz, s, LANES), a_vec, dtb_vec, odd_norm_w[j], tb)
            h = _out_proj(h, [o.reshape(t, 2 * d)], _bf(odd_w_out[j]), final_w, tm)
    return h.reshape(bsz, s, d)
```

```python
import functools

import jax
import jax.numpy as jnp
from jax import lax
from jax.experimental import pallas as pl
from jax.experimental.pallas import tpu as pltpu

F32 = jnp.float32
BF16 = jnp.bfloat16

RMS_EPS = 1e-6
LANES = 128
SUBLANES = 8
VMEM_LIMIT = 48 * 1024 * 1024

D_MODEL = 1024
A_HEAD_DIM = 64
A_BLOCK = 128
A_DILATIONS = (1, 4, 16)
CHUNK = 64
A_SCALE = A_HEAD_DIM ** -0.5 * 1.4426950408889634
A_MOD = 16
A_GROUP = 8
A_GROUP_FIRST = 16
HEAD = 128
CONV_K = 4
HGRN_KH = 2
GDN_KH = 8

EVEN_IN = 8 * D_MODEL
ODD_CONV = 4 * D_MODEL
QK_TILES = 2
ODD_REST_PAD = 2 * D_MODEL + LANES
N_V_HEADS = 16


def _dot(a, b):
    return jnp.dot(a, b, preferred_element_type=F32)


def _dot_nt(a, b, precision=None):
    return lax.dot_general(a, b, (((1,), (1,)), ((), ())), precision=precision,
                           preferred_element_type=F32)


def _dot_tn(a, b):
    return lax.dot_general(a, b, (((0,), (0,)), ((), ())), preferred_element_type=F32)


def _bf(x):
    return x.astype(BF16)


def _silu(x):
    return x * jax.nn.sigmoid(x)


def _softplus(x):
    return jnp.maximum(x, 0.0) + jnp.log1p(jnp.exp(-jnp.abs(x)))


def _params(*sem):
    return pltpu.CompilerParams(dimension_semantics=sem, vmem_limit_bytes=VMEM_LIMIT)


def _norm_rows(x_ref, nw_ref, xn_ref):
    x = x_ref[...]
    ms = jnp.mean(x * x, axis=-1, keepdims=True)
    xn_ref[...] = _bf(x * lax.rsqrt(ms + RMS_EPS) * nw_ref[...])


def _norm_matmul_kernel(x_ref, nw_ref, w_ref, o_ref, side_ref, xn_ref, *, side_tile, side_lo):
    j = pl.program_id(1)
    pl.when(j == 0)(functools.partial(_norm_rows, x_ref, nw_ref, xn_ref))
    acc = _dot(xn_ref[...], w_ref[...])
    o_ref[...] = _bf(acc)

    @pl.when(j == side_tile)
    def _():
        side_ref[...] = acc[:, side_lo:side_lo + side_ref.shape[1]]


def _norm_matmul(x, nw, w, tm, tn, side_start, side_width):
    t, d = x.shape
    n = w.shape[1]
    side_tile, side_lo = divmod(side_start, tn)
    assert side_lo + side_width <= tn
    return pl.pallas_call(
        functools.partial(_norm_matmul_kernel, side_tile=side_tile, side_lo=side_lo),
        grid=(t // tm, n // tn),
        in_specs=[pl.BlockSpec((tm, d), lambda i, j: (i, 0)),
                  pl.BlockSpec((1, d), lambda i, j: (0, 0)),
                  pl.BlockSpec((d, tn), lambda i, j: (0, j))],
        out_specs=[pl.BlockSpec((tm, tn), lambda i, j: (i, j)),
                   pl.BlockSpec((tm, side_width), lambda i, j: (i, 0))],
        out_shape=[jax.ShapeDtypeStruct((t, n), BF16),
                   jax.ShapeDtypeStruct((t, side_width), F32)],
        scratch_shapes=[pltpu.VMEM((tm, d), BF16)],
        compiler_params=_params("parallel", "arbitrary"),
        name="norm_in_proj",
    )(x, nw.reshape(1, d), w)


def _even_proj_kernel(x_ref, nw_ref, w_ref, loglb_ref, log1mlb_ref, o_ref, logf_ref, k_ref, xn_ref,
                      fbuf, *, n_tiles):
    j = pl.program_id(1)
    pl.when(j == 0)(functools.partial(_norm_rows, x_ref, nw_ref, xn_ref))
    acc = _dot(xn_ref[...], w_ref[...])
    o_ref[...] = _bf(acc)

    @pl.when(j == 0)
    def _():
        fbuf[...] = acc

    step_rows = fbuf.shape[0] // n_tiles
    rows = pl.ds(pl.multiple_of(j * step_rows, step_rows), step_rows)
    x = fbuf[rows, :]
    log_sig = jnp.minimum(x, 0.0) - jnp.log1p(jnp.exp(-jnp.abs(x)))
    log_lb = loglb_ref[...]
    b = log1mlb_ref[...] + log_sig
    logf_ref[rows, :] = jnp.maximum(log_lb, b) + jnp.log1p(jnp.exp(-jnp.abs(log_lb - b)))
    k_ref[rows, :] = _bf(jnp.exp(b - x))


def _even_proj(x, nw, w, log_lb, log_1mlb, tm):
    t, d = x.shape
    n = w.shape[1]
    n_tiles = n // d
    f_tile = 5

    def w_tile(i, j):
        return 0, jnp.where(j == 0, f_tile, jnp.where(j <= f_tile, j - 1, j))

    return pl.pallas_call(
        functools.partial(_even_proj_kernel, n_tiles=n_tiles),
        grid=(t // tm, n_tiles),
        in_specs=[pl.BlockSpec((tm, d), lambda i, j: (i, 0)),
                  pl.BlockSpec((1, d), lambda i, j: (0, 0)),
                  pl.BlockSpec((d, d), w_tile),
                  pl.BlockSpec((1, d), lambda i, j: (0, 0)),
                  pl.BlockSpec((1, d), lambda i, j: (0, 0))],
        out_specs=[pl.BlockSpec((tm, d), lambda i, j: (i, j)),
                   pl.BlockSpec((tm, d), lambda i, j: (i, 0)),
                   pl.BlockSpec((tm, d), lambda i, j: (i, 0))],
        out_shape=[jax.ShapeDtypeStruct((t, n), BF16),
                   jax.ShapeDtypeStruct((t, d), F32),
                   jax.ShapeDtypeStruct((t, d), BF16)],
        scratch_shapes=[pltpu.VMEM((tm, d), BF16), pltpu.VMEM((tm, d), F32)],
        compiler_params=_params("parallel", "arbitrary"),
        name="even_in_proj",
    )(x, nw.reshape(1, d), w, log_lb.reshape(1, d), log_1mlb.reshape(1, d))


def _qkv_conv_kernel(x_ref, nw_ref, w_ref, cw_ref, o_ref, xn_ref, halo_ref, *, tiles_per_seq):
    i = pl.program_id(0)
    j = pl.program_id(1)
    tm, tn = o_ref.shape
    halo = halo_ref.shape[1]

    pl.when(j == 0)(functools.partial(_norm_rows, x_ref, nw_ref, xn_ref))

    @pl.when(i == 0)
    def _():
        halo_ref[j] = jnp.zeros((halo, tn), F32)

    acc = _dot(xn_ref[...], w_ref[...])
    prev = jnp.where(i % tiles_per_seq == 0, 0.0, halo_ref[j])
    halo_ref[j] = acc[tm - halo:, :]
    ext = jnp.concatenate([prev, acc], axis=0)
    cw = cw_ref[...]
    conv = ext * cw[0:1, :]
    for k in range(1, CONV_K):
        conv = ext * cw[k:k + 1, :] + pltpu.roll(conv, 1, 0)
    y = _silu(conv[halo:, :])

    @pl.when(j < QK_TILES)
    def _():
        scale = jnp.where(j == 0, HEAD ** -0.5, 1.0)
        for h0 in range(0, tn, HEAD):
            yh = y[:, h0:h0 + HEAD]
            inv = lax.rsqrt(jnp.sum(yh * yh, axis=-1, keepdims=True) + RMS_EPS) * scale
            o_ref[:, h0:h0 + HEAD] = _bf(yh * inv)

    @pl.when(j >= QK_TILES)
    def _():
        o_ref[...] = _bf(y)


def _qkv_conv_proj(x, nw, w, conv_w, tm, seq):
    t, d = x.shape
    n = conv_w.shape[1]
    tn = d
    return pl.pallas_call(
        functools.partial(_qkv_conv_kernel, tiles_per_seq=seq // tm),
        grid=(t // tm, n // tn),
        in_specs=[pl.BlockSpec((tm, d), lambda i, j: (i, 0)),
                  pl.BlockSpec((1, d), lambda i, j: (0, 0)),
                  pl.BlockSpec((d, tn), lambda i, j: (0, j)),
                  pl.BlockSpec((CONV_K, tn), lambda i, j: (0, j))],
        out_specs=pl.BlockSpec((tm, tn), lambda i, j: (i, j)),
        out_shape=jax.ShapeDtypeStruct((t, n), BF16),
        scratch_shapes=[pltpu.VMEM((tm, d), BF16),
                        pltpu.VMEM((n // tn, SUBLANES, tn), F32)],
        compiler_params=_params("arbitrary", "arbitrary"),
        name="qkv_conv_proj",
    )(x, nw.reshape(1, d), w, conv_w)


def _out_proj_kernel(*refs, n_in, final):
    h_ref = refs[0]
    a_refs = refs[1:1 + n_in]
    w_refs = refs[1 + n_in:1 + 2 * n_in]
    o_ref = refs[-1]
    acc = h_ref[...]
    for a_ref, w_ref in zip(a_refs, w_refs):
        acc = acc + _dot(a_ref[...], w_ref[...])
    if final:
        fw_ref = refs[1 + 2 * n_in]
        ms = jnp.mean(acc * acc, axis=-1, keepdims=True)
        acc = acc * lax.rsqrt(ms + RMS_EPS) * fw_ref[...]
    o_ref[...] = acc


def _out_proj(h, acts, w, final_w, tm):
    t, d = h.shape
    n_in = len(acts)
    k = acts[0].shape[1]
    final = final_w is not None
    in_specs = [pl.BlockSpec((tm, d), lambda i: (i, 0))]
    in_specs += [pl.BlockSpec((tm, k), lambda i: (i, 0)) for _ in acts]
    in_specs += [pl.BlockSpec((k, d), lambda i, rb=rb: (rb, 0)) for rb in range(n_in)]
    args = [h, *acts] + [w] * n_in
    if final:
        in_specs.append(pl.BlockSpec((1, d), lambda i: (0, 0)))
        args.append(final_w.reshape(1, d))
    return pl.pallas_call(
        functools.partial(_out_proj_kernel, n_in=n_in, final=final),
        grid=(t // tm,),
        in_specs=in_specs,
        out_specs=pl.BlockSpec((tm, d), lambda i: (i, 0)),
        out_shape=jax.ShapeDtypeStruct((t, d), F32),
        compiler_params=_params("parallel"),
        name="out_proj",
    )(*args)


def _attn_kernel(q_ref, k_ref, v_ref, gate_ref, o_ref, qp_ref, kp_ref, vp_ref, acc_ref, m_ref, l_ref,
                 bias_ref, *, seq):
    per_res = seq // A_MOD
    n_blocks = seq // A_BLOCK
    row = lax.broadcasted_iota(jnp.int32, (A_BLOCK, 2 * A_BLOCK), 0)
    col = lax.broadcasted_iota(jnp.int32, (A_BLOCK, 2 * A_BLOCK), 1)
    lane = lax.broadcasted_iota(jnp.int32, (A_BLOCK, LANES), 1)
    head0 = lane < A_HEAD_DIM

    def widen(t, carry):
        rows = pl.ds(pl.multiple_of(t * A_BLOCK, A_BLOCK), A_BLOCK)
        acc_ref[rows, :] = q_ref[rows, :].astype(F32) * A_SCALE
        m_ref[rows, :] = k_ref[rows, :].astype(F32)
        l_ref[rows, :] = v_ref[rows, :].astype(F32)
        return carry

    lax.fori_loop(0, n_blocks, widen, 0)

    def to_residue_major(res, carry):
        src = pl.ds(res, per_res, stride=A_MOD)
        dst = pl.ds(pl.multiple_of(res * per_res, per_res), per_res)
        qp_ref[dst, :] = acc_ref[src, :]
        kp_ref[dst, :] = m_ref[src, :]
        vp_ref[dst, :] = l_ref[src, :]
        return carry

    lax.fori_loop(0, A_MOD, to_residue_major, 0)

    for dilation in A_DILATIONS:
        first = dilation == A_DILATIONS[0]
        per_class = n_blocks // dilation
        runs = A_MOD // dilation
        rq = A_BLOCK // runs
        dist = runs * (row % rq - col % (2 * rq)) + (row // rq - col // (2 * rq)) + A_BLOCK
        band = (dist >= 0) & (dist <= A_BLOCK)
        in_cur = col % (2 * rq) >= rq
        slot = 2 * A_DILATIONS.index(dilation)
        bias_ref[slot] = jnp.where(band & in_cur, 0.0, -jnp.inf)
        bias_ref[slot + 1] = jnp.where(band, 0.0, -jnp.inf)

        def load(n, dilation=dilation, first=first, per_class=per_class, runs=runs, rq=rq):
            res = n // per_class
            i = n % per_class
            cur, prev = [], []
            for c in range(runs):
                base = pl.multiple_of((c * dilation + res) * per_res + rq * i, SUBLANES)
                cur.append(pl.ds(base, rq))
                prev.append(pl.ds(pl.multiple_of(jnp.where(i > 0, base - rq, base), SUBLANES), rq))

            def window(ref):
                return _bf(jnp.concatenate([ref[rows, :] for pc in zip(prev, cur) for rows in pc], axis=0))

            def block(ref):
                return jnp.concatenate([ref[rows, :] for rows in cur], axis=0)

            return cur, i, block(qp_ref), window(kp_ref), window(vp_ref)

        def scores(i, q, kw, vw, slot=slot):
            bias = bias_ref[slot + jnp.where(i > 0, 1, 0)]
            parts = []
            for h in range(2):
                qh = _bf(jnp.where(head0 if h == 0 else ~head0, q, 0.0))
                s = _dot_nt(qh, kw) + bias
                m_blk = jnp.max(s, axis=-1, keepdims=True)
                p = jnp.exp2(s - m_blk)
                parts.append((_dot(_bf(p), vw), m_blk, jnp.sum(p, axis=-1, keepdims=True)))
            return tuple(jnp.where(head0, a, b) for a, b in zip(*parts))

        n_group = A_GROUP_FIRST if first else A_GROUP

        def group(t, carry, load=load, scores=scores, first=first, rq=rq, n_group=n_group):
            loaded = [load(t * n_group + u) for u in range(n_group)]
            results = [scores(*item[1:]) for item in loaded]
            for (cur, *_), (acc, m_blk, l_blk) in zip(loaded, results):
                if not first:
                    acc_old, m_old, l_old = (
                        jnp.concatenate([ref[rows, :] for rows in cur], axis=0)
                        for ref in (acc_ref, m_ref, l_ref))
                    m_new = jnp.maximum(m_old, m_blk)
                    alpha = jnp.exp2(m_old - m_new)
                    beta = jnp.exp2(m_blk - m_new)
                    acc = acc_old * alpha + acc * beta
                    l_blk = l_old * alpha + l_blk * beta
                    m_blk = m_new
                for ref, val in zip((acc_ref, m_ref, l_ref), (acc, m_blk, l_blk)):
                    for c, rows in enumerate(cur):
                        ref[rows, :] = val[c * rq:(c + 1) * rq, :]
            return carry

        lax.fori_loop(0, n_blocks // n_group, group, 0)

    def to_position_order(res, carry):
        src = pl.ds(pl.multiple_of(res * per_res, per_res), per_res)
        m_ref[pl.ds(res, per_res, stride=A_MOD), :] = acc_ref[src, :] / l_ref[src, :]
        return carry

    lax.fori_loop(0, A_MOD, to_position_order, 0)

    def finish(t, carry):
        rows = pl.ds(pl.multiple_of(t * A_BLOCK, A_BLOCK), A_BLOCK)
        o_ref[rows, :] = _bf(m_ref[rows, :] * _silu(gate_ref[rows, :].astype(F32)))
        return carry

    lax.fori_loop(0, n_blocks, finish, 0)


def _dilated_attention(proj):
    bsz, s, _ = proj.shape
    hp_blocks = D_MODEL // LANES

    def col_spec(off):
        return pl.BlockSpec((None, s, LANES), lambda b, hp: (b, 0, (off + 1) * hp_blocks + hp))

    return pl.pallas_call(
        functools.partial(_attn_kernel, seq=s),
        grid=(bsz, hp_blocks),
        in_specs=[col_spec(0), col_spec(1), col_spec(2), col_spec(3)],
        out_specs=pl.BlockSpec((None, s, LANES), lambda b, hp: (b, 0, hp)),
        out_shape=jax.ShapeDtypeStruct((bsz, s, D_MODEL), BF16),
        scratch_shapes=[pltpu.VMEM((s, LANES), F32)] * 6
        + [pltpu.VMEM((2 * len(A_DILATIONS), A_BLOCK, 2 * A_BLOCK), F32)],
        compiler_params=_params("parallel", "parallel"),
        name="dilated_attn",
    )(proj, proj, proj, proj)


def _hgrn_kernel(q_ref, logf_ref, k_ref, i_ref, gate_ref, nw_ref, o_ref, st_ref, *, tb):
    @pl.when(pl.program_id(2) == 0)
    def _():
        st_ref[...] = jnp.zeros_like(st_ref)

    n_ch = tb // CHUNK
    r64 = lax.broadcasted_iota(jnp.int32, (CHUNK, CHUNK), 0)
    c64 = lax.broadcasted_iota(jnp.int32, (CHUNK, CHUNK), 1)
    tri = _bf(jnp.where(r64 >= c64, 1.0, 0.0))
    row = lax.broadcasted_iota(jnp.int32, (CHUNK, HEAD), 0)
    sub = lax.broadcasted_iota(jnp.int32, (SUBLANES, HEAD), 0)
    units = [(hd, slice(c * CHUNK, (c + 1) * CHUNK), slice(hd * HEAD, (hd + 1) * HEAD))
             for c in range(n_ch) for hd in range(HGRN_KH)]
    n_units = len(units)
    qs = [q_ref[rows, cols].astype(F32) for _, rows, cols in units]
    ks = [k_ref[rows, cols].astype(F32) for _, rows, cols in units]
    vbs = [i_ref[rows, cols] for _, rows, cols in units]

    gs = []
    for _, rows, cols in units:
        lf = logf_ref[rows, cols]
        hi = _bf(lf)
        r1 = lf - hi.astype(F32)
        mid = _bf(r1)
        lo = _bf(r1 - mid.astype(F32))
        cum = _dot(tri, jnp.concatenate([hi, mid, lo], axis=1))
        gs.append(cum[:, :HEAD] + cum[:, HEAD:2 * HEAD] + cum[:, 2 * HEAD:])

    attn = [None] * n_units
    s = CHUNK // 2
    while s >= 1:
        blk = 2 * s
        upper = (row % blk) >= s
        mask = ((r64 // blk) == (c64 // blk)) & ((r64 % blk) >= s) & ((c64 % blk) < s)
        for c in range(n_units):
            g = gs[c]
            if blk >= SUBLANES:
                g_ref = jnp.concatenate(
                    [jnp.broadcast_to(g[b0 + s - 1:b0 + s, :], (blk, HEAD)) for b0 in range(0, CHUNK, blk)],
                    axis=0)
            elif s == 2:
                g_ref = jnp.concatenate(
                    [jnp.where(sub < 4, jnp.broadcast_to(g[v0 + 1:v0 + 2, :], (SUBLANES, HEAD)),
                               jnp.broadcast_to(g[v0 + 5:v0 + 6, :], (SUBLANES, HEAD)))
                     for v0 in range(0, CHUNK, SUBLANES)], axis=0)
            else:
                g_ref = jnp.where(upper, pltpu.roll(g, 1, 0), g)
            z = _bf(jnp.where(upper, qs[c], ks[c]) * jnp.exp(-jnp.abs(g - g_ref)))
            m = _dot_nt(z, z)
            attn[c] = jnp.where(mask, m, 0.0) if attn[c] is None else jnp.where(mask, m, attn[c])
        s //= 2
    eye = r64 == c64
    attn = [jnp.where(eye, jnp.sum(qs[c] * ks[c], axis=-1, keepdims=True), attn[c]) for c in range(n_units)]

    o_intra = [_dot(_bf(attn[c]), vbs[c]) for c in range(n_units)]
    q_dec = [_bf(qs[c] * jnp.exp(gs[c])) for c in range(n_units)]
    g_last = [gs[c][CHUNK - 1:CHUNK, :] for c in range(n_units)]
    kv = [_dot_tn(vbs[c], _bf(ks[c] * jnp.exp(g_last[c] - gs[c]))) for c in range(n_units)]

    states = [st_ref[hd] for hd in range(HGRN_KH)]
    outs = []
    for c, (hd, _, _) in enumerate(units):
        outs.append(_dot_nt(q_dec[c], _bf(states[hd])) + o_intra[c])
        states[hd] = states[hd] * jnp.exp(g_last[c]) + kv[c]
    for hd in range(HGRN_KH):
        st_ref[hd] = states[hd]

    for c, (_, rows, cols) in enumerate(units):
        o = outs[c]
        ms = jnp.mean(o * o, axis=-1, keepdims=True)
        o = o * lax.rsqrt(ms + RMS_EPS) * nw_ref[...]
        o_ref[rows, cols] = _bf(o * _silu(gate_ref[rows, cols].astype(F32)))


def _hgrn(proj, log_f, k, norm_w, tb):
    bsz, s, _ = proj.shape
    width = HGRN_KH * HEAD
    groups = D_MODEL // width
    base = 5 * groups

    def col_spec(off):
        return pl.BlockSpec((None, tb, width), lambda b, h, t: (b, t, base + off * groups + h))

    head_spec = pl.BlockSpec((None, tb, width), lambda b, h, t: (b, t, h))
    return pl.pallas_call(
        functools.partial(_hgrn_kernel, tb=tb),
        grid=(bsz, groups, s // tb),
        in_specs=[col_spec(0), head_spec, head_spec, col_spec(1), col_spec(2),
                  pl.BlockSpec((1, HEAD), lambda b, h, t: (0, 0))],
        out_specs=head_spec,
        out_shape=jax.ShapeDtypeStruct((bsz, s, D_MODEL), BF16),
        scratch_shapes=[pltpu.VMEM((HGRN_KH, HEAD, HEAD), F32)],
        compiler_params=_params("parallel", "parallel", "arbitrary"),
        name="hgrn2",
    )(proj, log_f, k, proj, proj, norm_w.reshape(1, HEAD))


def _unit_lower_inverses(lows):
    r = lax.broadcasted_iota(jnp.int32, (CHUNK, CHUNK), 0)
    c = lax.broadcasted_iota(jnp.int32, (CHUNK, CHUNK), 1)
    eye = jnp.where(r == c, 1.0, 0.0).astype(F32)
    s = 1
    invs = None
    while s < CHUNK:
        sel = ((r // (2 * s)) == (c // (2 * s))) & ((r % (2 * s)) >= s) & ((c % (2 * s)) < s)
        low_s = [jnp.where(sel, low, 0.0) for low in lows]
        if s == 1:
            invs = [eye - x for x in low_s]
        else:
            inv_b = [_bf(x) for x in invs]
            tmp = [_bf(_dot(a, _bf(x))) for a, x in zip(inv_b, low_s)]
            invs = [x - _dot(t, a) for x, t, a in zip(invs, tmp, inv_b)]
        s *= 2
    return invs


def _gdn_kernel(q_ref, k_ref, v_ref, z_ref, gates_ref, avec_ref, dtb_ref, nw_ref, o_ref, st_ref, *, tb):
    tt = pl.program_id(2)
    group = pl.program_id(1)
    n_ch = tb // CHUNK

    @pl.when(tt == 0)
    def _():
        st_ref[...] = jnp.zeros_like(st_ref)

    q_all = [q_ref[:, kk * HEAD:(kk + 1) * HEAD].astype(F32) for kk in range(GDN_KH)]
    k_all = [k_ref[:, kk * HEAD:(kk + 1) * HEAD].astype(F32) for kk in range(GDN_KH)]

    gl = gates_ref[...]
    beta_all = jax.nn.sigmoid(gl)
    g_all = avec_ref[...] * _softplus(gl + dtb_ref[...])
    lane = lax.broadcasted_iota(jnp.int32, (tb, LANES), 1)

    r64 = lax.broadcasted_iota(jnp.int32, (CHUNK, CHUNK), 0)
    c64 = lax.broadcasted_iota(jnp.int32, (CHUNK, CHUNK), 1)
    tri = _bf(jnp.where(r64 >= c64, 1.0, 0.0))
    causal = r64 >= c64
    strict = r64 > c64
    strict_f = jnp.where(strict, 1.0, 0.0).astype(F32)

    chunk_rows = [slice(c * CHUNK, (c + 1) * CHUNK) for c in range(n_ch)]
    n_vh = 2 * GDN_KH
    units = [(vh, c) for c in range(n_ch) for vh in range(n_vh)]

    beta_cols, g_cols = [], []
    for vh in range(n_vh):
        head = n_vh * group + vh
        beta_cols.append(jnp.sum(jnp.where(lane == head, beta_all, 0.0), axis=-1, keepdims=True))
        g_cols.append(jnp.sum(jnp.where(lane == N_V_HEADS + head, g_all, 0.0), axis=-1, keepdims=True))

    kq = [[_dot_nt(_bf(jnp.concatenate([k_all[kk][rows, :], q_all[kk][rows, :]], axis=0)),
                   _bf(k_all[kk][rows, :])) for rows in chunk_rows]
          for kk in range(GDN_KH)]

    decay, g_i, g_last = [], [], []
    for vh, c in units:
        gb = jnp.broadcast_to(g_cols[vh][chunk_rows[c], :], (CHUNK, CHUNK))
        wmat = jnp.concatenate([gb * strict_f, gb], axis=1)
        hi = _bf(wmat)
        lo = _bf(wmat - hi.astype(F32))
        cum = _dot(tri, hi) + _dot(tri, lo)
        decay.append(jnp.where(causal, jnp.exp(jnp.minimum(cum[:, :CHUNK], 0.0)), 0.0))
        g_i.append(cum[:, CHUNK:CHUNK + 1])
        g_last.append(cum[CHUNK - 1:CHUNK, CHUNK:CHUNK + 1])

    lows = [jnp.where(strict, kq[vh // 2][c][:CHUNK, :] * beta_cols[vh][chunk_rows[c], :] * decay[n], 0.0)
            for n, (vh, c) in enumerate(units)]
    invs = _unit_lower_inverses(lows)

    sols = []
    for n, (vh, c) in enumerate(units):
        rows = chunk_rows[c]
        beta = beta_cols[vh][rows, :]
        v = v_ref[rows, vh * HEAD:(vh + 1) * HEAD].astype(F32)
        rhs = jnp.concatenate([v * beta, k_all[vh // 2][rows, :] * (beta * jnp.exp(g_i[n]))], axis=1)
        sols.append(_bf(_dot(_bf(invs[n]), _bf(rhs))))

    o_loc, q_eff, c_mat, p_mat = [], [], [], []
    for n, (vh, c) in enumerate(units):
        rows = chunk_rows[c]
        attn = _bf(kq[vh // 2][c][CHUNK:, :] * decay[n])
        auw = _dot(attn, sols[n])
        o_loc.append(auw[:, :HEAD])
        q_eff.append(_bf(q_all[vh // 2][rows, :] * jnp.exp(g_i[n]) - auw[:, HEAD:]))
        k_tail = _bf(k_all[vh // 2][rows, :] * jnp.exp(g_last[n] - g_i[n]))
        ktuw = _dot_tn(k_tail, sols[n])
        c_mat.append(ktuw[:, :HEAD])
        p_mat.append(_bf(ktuw[:, HEAD:]))

    states = [st_ref[vh] for vh in range(n_vh)]
    outs = [None] * len(units)
    for n, (vh, c) in enumerate(units):
        sb = _bf(states[vh])
        outs[n] = _dot(q_eff[n], sb) + o_loc[n]
        states[vh] = states[vh] * jnp.exp(g_last[n]) + c_mat[n] - _dot(p_mat[n], sb)
    for vh in range(n_vh):
        st_ref[vh] = states[vh]

    for n, (vh, c) in enumerate(units):
        rows = chunk_rows[c]
        o = outs[n]
        ms = jnp.mean(o * o, axis=-1, keepdims=True)
        o = o * lax.rsqrt(ms + RMS_EPS) * nw_ref[...]
        z = z_ref[rows, vh * HEAD:(vh + 1) * HEAD].astype(F32)
        o_ref[rows, vh * HEAD:(vh + 1) * HEAD] = _bf(o * _silu(z))


def _gdn(qkv, z, gate_logits, a_vec, dtb_vec, norm_w, tb):
    bsz, s, _ = qkv.shape
    width = GDN_KH * HEAD
    groups = D_MODEL // width

    in_specs = [
        pl.BlockSpec((None, tb, width), lambda b, h, t: (b, t, h)),
        pl.BlockSpec((None, tb, width), lambda b, h, t: (b, t, groups + h)),
        pl.BlockSpec((None, tb, 2 * width), lambda b, h, t: (b, t, groups + h)),
        pl.BlockSpec((None, tb, 2 * width), lambda b, h, t: (b, t, h)),
        pl.BlockSpec((None, tb, LANES), lambda b, h, t: (b, t, 0)),
        pl.BlockSpec((1, LANES), lambda b, h, t: (0, 0)),
        pl.BlockSpec((1, LANES), lambda b, h, t: (0, 0)),
        pl.BlockSpec((1, HEAD), lambda b, h, t: (0, 0)),
    ]
    return pl.pallas_call(
        functools.partial(_gdn_kernel, tb=tb),
        grid=(bsz, groups, s // tb),
        in_specs=in_specs,
        out_specs=pl.BlockSpec((None, tb, 2 * width), lambda b, h, t: (b, t, h)),
        out_shape=jax.ShapeDtypeStruct((bsz, s, 2 * D_MODEL), BF16),
        scratch_shapes=[pltpu.VMEM((2 * GDN_KH, HEAD, HEAD), F32)],
        compiler_params=_params("parallel", "parallel", "arbitrary"),
        name="gated_deltanet",
    )(qkv, qkv, qkv, z, gate_logits, a_vec, dtb_vec, norm_w.reshape(1, HEAD))


def kernel(x, norm_w, final_norm_w, even_w_in, even_w_out, hgrn_lb_logits, hgrn_norm_w,
           odd_w_in, odd_conv_w, odd_dt_bias, odd_a_log, odd_norm_w, odd_w_out):
    bsz, s, d = x.shape
    depth = norm_w.shape[0]
    t = bsz * s
    tm = 1024
    tm_in = 1024
    tb = 512

    lb_all = jnp.cumsum(jax.nn.softmax(hgrn_lb_logits.astype(F32), axis=0), axis=0)
    lb_all = jnp.maximum(lb_all - lb_all[0:1], 0.0)
    log_lb = jnp.log(lb_all)
    log_1mlb = jnp.log1p(-lb_all)

    h = x.reshape(t, d)
    for layer in range(depth):
        j = layer // 2
        final_w = final_norm_w if layer == depth - 1 else None
        if layer % 2 == 0:
            proj, log_f, k = _even_proj(h, norm_w[layer], _bf(even_w_in[j]), log_lb[j], log_1mlb[j],
                                        tm_in)
            proj = proj.reshape(bsz, s, EVEN_IN)
            a_mix = _dilated_attention(proj)
            b_mix = _hgrn(proj, log_f.reshape(bsz, s, d), k.reshape(bsz, s, d), hgrn_norm_w[j], 2 * tb)
            h = _out_proj(h, [a_mix.reshape(t, d), b_mix.reshape(t, d)], _bf(even_w_out[j]),
                          final_w, tm)
        else:
            w_in = _bf(odd_w_in[j])
            pad = jnp.zeros((d, ODD_REST_PAD - (w_in.shape[1] - ODD_CONV)), BF16)
            w_rest = jnp.concatenate([w_in[:, ODD_CONV:], pad], axis=1)
            qkv = _qkv_conv_proj(h, norm_w[layer], w_in, odd_conv_w[j], tm_in, s)
            z, gate_logits = _norm_matmul(h, norm_w[layer], w_rest, tm_in, ODD_REST_PAD,
                                          side_start=ODD_REST_PAD - LANES, side_width=LANES)
            zeros16 = jnp.zeros((N_V_HEADS,), F32)
            tail = jnp.zeros((LANES - 2 * N_V_HEADS,), F32)
            a_vec = jnp.concatenate([zeros16, -jnp.exp(odd_a_log[j].astype(F32)), tail]).reshape(1, LANES)
            dtb_vec = jnp.concatenate([zeros16, odd_dt_bias[j].astype(F32), tail]).reshape(1, LANES)
            o = _gdn(qkv.reshape(bsz, s, ODD_CONV), z.reshape(bsz, s, ODD_REST_PAD),
                     gate_logits.reshape(bsz, s, LANES), a_vec, dtb_vec, odd_norm_w[j], tb // 4)
            h = _out_proj(h, [o.reshape(t, 2 * d)], _bf(odd_w_out[j]), final_w, tm)
    return h.reshape(bsz, s, d)
```

```python
import functools

import jax
import jax.numpy as jnp
from jax import lax
from jax.experimental import pallas as pl
from jax.experimental.pallas import tpu as pltpu

F32 = jnp.float32
BF16 = jnp.bfloat16

RMS_EPS = 1e-6
LANES = 128
SUBLANES = 8
VMEM_LIMIT = 48 * 1024 * 1024

D_MODEL = 1024
A_HEAD_DIM = 64
A_BLOCK = 128
A_DILATIONS = (1, 4, 16)
CHUNK = 64
A_SCALE = A_HEAD_DIM ** -0.5 * 1.4426950408889634
A_MOD = 16
A_GROUP = 8
A_GROUP_FIRST = 16
HEAD = 128
CONV_K = 4
HGRN_KH = 8
GDN_KH = 8

EVEN_IN = 8 * D_MODEL
ODD_CONV = 4 * D_MODEL
QK_TILES = 2
ODD_REST_PAD = 2 * D_MODEL + LANES
N_V_HEADS = 16


def _dot(a, b):
    return jnp.dot(a, b, preferred_element_type=F32)


def _dot_nt(a, b, precision=None):
    return lax.dot_general(a, b, (((1,), (1,)), ((), ())), precision=precision,
                           preferred_element_type=F32)


def _dot_tn(a, b):
    return lax.dot_general(a, b, (((0,), (0,)), ((), ())), preferred_element_type=F32)


def _bf(x):
    return x.astype(BF16)


def _silu(x):
    return x * jax.nn.sigmoid(x)


def _softplus(x):
    return jnp.maximum(x, 0.0) + jnp.log1p(jnp.exp(-jnp.abs(x)))


def _params(*sem):
    return pltpu.CompilerParams(dimension_semantics=sem, vmem_limit_bytes=VMEM_LIMIT)


def _norm_rows(x_ref, nw_ref, xn_ref):
    x = x_ref[...]
    ms = jnp.mean(x * x, axis=-1, keepdims=True)
    xn_ref[...] = _bf(x * lax.rsqrt(ms + RMS_EPS) * nw_ref[...])


def _norm_matmul_kernel(x_ref, nw_ref, w_ref, o_ref, side_ref, xn_ref, *, side_tile, side_lo):
    j = pl.program_id(1)
    pl.when(j == 0)(functools.partial(_norm_rows, x_ref, nw_ref, xn_ref))
    acc = _dot(xn_ref[...], w_ref[...])
    o_ref[...] = _bf(acc)

    @pl.when(j == side_tile)
    def _():
        side_ref[...] = acc[:, side_lo:side_lo + side_ref.shape[1]]


def _norm_matmul(x, nw, w, tm, tn, side_start, side_width):
    t, d = x.shape
    n = w.shape[1]
    side_tile, side_lo = divmod(side_start, tn)
    assert side_lo + side_width <= tn
    return pl.pallas_call(
        functools.partial(_norm_matmul_kernel, side_tile=side_tile, side_lo=side_lo),
        grid=(t // tm, n // tn),
        in_specs=[pl.BlockSpec((tm, d), lambda i, j: (i, 0)),
                  pl.BlockSpec((1, d), lambda i, j: (0, 0)),
                  pl.BlockSpec((d, tn), lambda i, j: (0, j))],
        out_specs=[pl.BlockSpec((tm, tn), lambda i, j: (i, j)),
                   pl.BlockSpec((tm, side_width), lambda i, j: (i, 0))],
        out_shape=[jax.ShapeDtypeStruct((t, n), BF16),
                   jax.ShapeDtypeStruct((t, side_width), F32)],
        scratch_shapes=[pltpu.VMEM((tm, d), BF16)],
        compiler_params=_params("parallel", "arbitrary"),
        name="norm_in_proj",
    )(x, nw.reshape(1, d), w)


def _even_proj_kernel(x_ref, nw_ref, w_ref, loglb_ref, log1mlb_ref, o_ref, logf_ref, k_ref, xn_ref,
                      fbuf, *, n_tiles):
    j = pl.program_id(1)
    pl.when(j == 0)(functools.partial(_norm_rows, x_ref, nw_ref, xn_ref))
    acc = _dot(xn_ref[...], w_ref[...])
    o_ref[...] = _bf(acc)

    @pl.when(j == 0)
    def _():
        fbuf[...] = acc

    step_rows = fbuf.shape[0] // n_tiles
    rows = pl.ds(pl.multiple_of(j * step_rows, step_rows), step_rows)
    x = fbuf[rows, :]
    log_sig = jnp.minimum(x, 0.0) - jnp.log1p(jnp.exp(-jnp.abs(x)))
    log_lb = loglb_ref[...]
    b = log1mlb_ref[...] + log_sig
    logf_ref[rows, :] = jnp.maximum(log_lb, b) + jnp.log1p(jnp.exp(-jnp.abs(log_lb - b)))
    k_ref[rows, :] = _bf(jnp.exp(b - x))


def _even_proj(x, nw, w, log_lb, log_1mlb, tm):
    t, d = x.shape
    n = w.shape[1]
    n_tiles = n // d
    f_tile = 5

    def w_tile(i, j):
        return 0, jnp.where(j == 0, f_tile, jnp.where(j <= f_tile, j - 1, j))

    return pl.pallas_call(
        functools.partial(_even_proj_kernel, n_tiles=n_tiles),
        grid=(t // tm, n_tiles),
        in_specs=[pl.BlockSpec((tm, d), lambda i, j: (i, 0)),
                  pl.BlockSpec((1, d), lambda i, j: (0, 0)),
                  pl.BlockSpec((d, d), w_tile),
                  pl.BlockSpec((1, d), lambda i, j: (0, 0)),
                  pl.BlockSpec((1, d), lambda i, j: (0, 0))],
        out_specs=[pl.BlockSpec((tm, d), lambda i, j: (i, j)),
                   pl.BlockSpec((tm, d), lambda i, j: (i, 0)),
                   pl.BlockSpec((tm, d), lambda i, j: (i, 0))],
        out_shape=[jax.ShapeDtypeStruct((t, n), BF16),
                   jax.ShapeDtypeStruct((t, d), F32),
                   jax.ShapeDtypeStruct((t, d), BF16)],
        scratch_shapes=[pltpu.VMEM((tm, d), BF16), pltpu.VMEM((tm, d), F32)],
        compiler_params=_params("parallel", "arbitrary"),
        name="even_in_proj",
    )(x, nw.reshape(1, d), w, log_lb.reshape(1, d), log_1mlb.reshape(1, d))


def _qkv_conv_kernel(x_ref, nw_ref, w_ref, cw_ref, o_ref, xn_ref, halo_ref, *, tiles_per_seq):
    i = pl.program_id(0)
    j = pl.program_id(1)
    tm, tn = o_ref.shape
    halo = halo_ref.shape[1]

    pl.when(j == 0)(functools.partial(_norm_rows, x_ref, nw_ref, xn_ref))

    @pl.when(i == 0)
    def _():
        halo_ref[j] = jnp.zeros((halo, tn), F32)

    acc = _dot(xn_ref[...], w_ref[...])
    prev = jnp.where(i % tiles_per_seq == 0, 0.0, halo_ref[j])
    halo_ref[j] = acc[tm - halo:, :]
    ext = jnp.concatenate([prev, acc], axis=0)
    cw = cw_ref[...]
    conv = ext * cw[0:1, :]
    for k in range(1, CONV_K):
        conv = ext * cw[k:k + 1, :] + pltpu.roll(conv, 1, 0)
    y = _silu(conv[halo:, :])

    @pl.when(j < QK_TILES)
    def _():
        scale = jnp.where(j == 0, HEAD ** -0.5, 1.0)
        for h0 in range(0, tn, HEAD):
            yh = y[:, h0:h0 + HEAD]
            inv = lax.rsqrt(jnp.sum(yh * yh, axis=-1, keepdims=True) + RMS_EPS) * scale
            o_ref[:, h0:h0 + HEAD] = _bf(yh * inv)

    @pl.when(j >= QK_TILES)
    def _():
        o_ref[...] = _bf(y)


def _qkv_conv_proj(x, nw, w, conv_w, tm, seq):
    t, d = x.shape
    n = conv_w.shape[1]
    tn = d
    return pl.pallas_call(
        functools.partial(_qkv_conv_kernel, tiles_per_seq=seq // tm),
        grid=(t // tm, n // tn),
        in_specs=[pl.BlockSpec((tm, d), lambda i, j: (i, 0)),
                  pl.BlockSpec((1, d), lambda i, j: (0, 0)),
                  pl.BlockSpec((d, tn), lambda i, j: (0, j)),
                  pl.BlockSpec((CONV_K, tn), lambda i, j: (0, j))],
        out_specs=pl.BlockSpec((tm, tn), lambda i, j: (i, j)),
        out_shape=jax.ShapeDtypeStruct((t, n), BF16),
        scratch_shapes=[pltpu.VMEM((tm, d), BF16),
                        pltpu.VMEM((n // tn, SUBLANES, tn), F32)],
        compiler_params=_params("arbitrary", "arbitrary"),
        name="qkv_conv_proj",
    )(x, nw.reshape(1, d), w, conv_w)


def _out_proj_kernel(*refs, n_in, final):
    h_ref = refs[0]
    a_refs = refs[1:1 + n_in]
    w_refs = refs[1 + n_in:1 + 2 * n_in]
    o_ref = refs[-1]
    acc = h_ref[...]
    for a_ref, w_ref in zip(a_refs, w_refs):
        acc = acc + _dot(a_ref[...], w_ref[...])
    if final:
        fw_ref = refs[1 + 2 * n_in]
        ms = jnp.mean(acc * acc, axis=-1, keepdims=True)
        acc = acc * lax.rsqrt(ms + RMS_EPS) * fw_ref[...]
    o_ref[...] = acc


def _out_proj(h, acts, w, final_w, tm):
    t, d = h.shape
    n_in = len(acts)
    k = acts[0].shape[1]
    final = final_w is not None
    in_specs = [pl.BlockSpec((tm, d), lambda i: (i, 0))]
    in_specs += [pl.BlockSpec((tm, k), lambda i: (i, 0)) for _ in acts]
    in_specs += [pl.BlockSpec((k, d), lambda i, rb=rb: (rb, 0)) for rb in range(n_in)]
    args = [h, *acts] + [w] * n_in
    if final:
        in_specs.append(pl.BlockSpec((1, d), lambda i: (0, 0)))
        args.append(final_w.reshape(1, d))
    return pl.pallas_call(
        functools.partial(_out_proj_kernel, n_in=n_in, final=final),
        grid=(t // tm,),
        in_specs=in_specs,
        out_specs=pl.BlockSpec((tm, d), lambda i: (i, 0)),
        out_shape=jax.ShapeDtypeStruct((t, d), F32),
        compiler_params=_params("parallel"),
        name="out_proj",
    )(*args)


def _attn_kernel(q_ref, k_ref, v_ref, gate_ref, o_ref, qp_ref, kp_ref, vp_ref, acc_ref, m_ref, l_ref,
                 bias_ref, *, seq):
    per_res = seq // A_MOD
    n_blocks = seq // A_BLOCK
    row = lax.broadcasted_iota(jnp.int32, (A_BLOCK, 2 * A_BLOCK), 0)
    col = lax.broadcasted_iota(jnp.int32, (A_BLOCK, 2 * A_BLOCK), 1)
    lane = lax.broadcasted_iota(jnp.int32, (A_BLOCK, LANES), 1)
    head0 = lane < A_HEAD_DIM

    def widen(t, carry):
        rows = pl.ds(pl.multiple_of(t * A_BLOCK, A_BLOCK), A_BLOCK)
        acc_ref[rows, :] = q_ref[rows, :].astype(F32) * A_SCALE
        m_ref[rows, :] = k_ref[rows, :].astype(F32)
        l_ref[rows, :] = v_ref[rows, :].astype(F32)
        return carry

    lax.fori_loop(0, n_blocks, widen, 0)

    def to_residue_major(res, carry):
        src = pl.ds(res, per_res, stride=A_MOD)
        dst = pl.ds(pl.multiple_of(res * per_res, per_res), per_res)
        qp_ref[dst, :] = acc_ref[src, :]
        kp_ref[dst, :] = m_ref[src, :]
        vp_ref[dst, :] = l_ref[src, :]
        return carry

    lax.fori_loop(0, A_MOD, to_residue_major, 0)

    for dilation in A_DILATIONS:
        first = dilation == A_DILATIONS[0]
        per_class = n_blocks // dilation
        runs = A_MOD // dilation
        rq = A_BLOCK // runs
        dist = runs * (row % rq - col % (2 * rq)) + (row // rq - col // (2 * rq)) + A_BLOCK
        band = (dist >= 0) & (dist <= A_BLOCK)
        in_cur = col % (2 * rq) >= rq
        slot = 2 * A_DILATIONS.index(dilation)
        bias_ref[slot] = jnp.where(band & in_cur, 0.0, -jnp.inf)
        bias_ref[slot + 1] = jnp.where(band, 0.0, -jnp.inf)

        def load(n, dilation=dilation, first=first, per_class=per_class, runs=runs, rq=rq):
            res = n // per_class
            i = n % per_class
            cur, prev = [], []
            for c in range(runs):
                base = pl.multiple_of((c * dilation + res) * per_res + rq * i, SUBLANES)
                cur.append(pl.ds(base, rq))
                prev.append(pl.ds(pl.multiple_of(jnp.where(i > 0, base - rq, base), SUBLANES), rq))

            def window(ref):
                return _bf(jnp.concatenate([ref[rows, :] for pc in zip(prev, cur) for rows in pc], axis=0))

            def block(ref):
                return jnp.concatenate([ref[rows, :] for rows in cur], axis=0)

            return cur, i, block(qp_ref), window(kp_ref), window(vp_ref)

        def scores(i, q, kw, vw, slot=slot):
            bias = bias_ref[slot + jnp.where(i > 0, 1, 0)]
            parts = []
            for h in range(2):
                qh = _bf(jnp.where(head0 if h == 0 else ~head0, q, 0.0))
                s = _dot_nt(qh, kw) + bias
                m_blk = jnp.max(s, axis=-1, keepdims=True)
                p = jnp.exp2(s - m_blk)
                parts.append((_dot(_bf(p), vw), m_blk, jnp.sum(p, axis=-1, keepdims=True)))
            return tuple(jnp.where(head0, a, b) for a, b in zip(*parts))

        n_group = A_GROUP_FIRST if first else A_GROUP

        def group(t, carry, load=load, scores=scores, first=first, rq=rq, n_group=n_group):
            loaded = [load(t * n_group + u) for u in range(n_group)]
            results = [scores(*item[1:]) for item in loaded]
            for (cur, *_), (acc, m_blk, l_blk) in zip(loaded, results):
                if not first:
                    acc_old, m_old, l_old = (
                        jnp.concatenate([ref[rows, :] for rows in cur], axis=0)
                        for ref in (acc_ref, m_ref, l_ref))
                    m_new = jnp.maximum(m_old, m_blk)
                    alpha = jnp.exp2(m_old - m_new)
                    beta = jnp.exp2(m_blk - m_new)
                    acc = acc_old * alpha + acc * beta
                    l_blk = l_old * alpha + l_blk * beta
                    m_blk = m_new
                for ref, val in zip((acc_ref, m_ref, l_ref), (acc, m_blk, l_blk)):
                    for c, rows in enumerate(cur):
                        ref[rows, :] = val[c * rq:(c + 1) * rq, :]
            return carry

        lax.fori_loop(0, n_blocks // n_group, group, 0)

    def to_position_order(res, carry):
        src = pl.ds(pl.multiple_of(res * per_res, per_res), per_res)
        m_ref[pl.ds(res, per_res, stride=A_MOD), :] = acc_ref[src, :] / l_ref[src, :]
        return carry

    lax.fori_loop(0, A_MOD, to_position_order, 0)

    def finish(t, carry):
        rows = pl.ds(pl.multiple_of(t * A_BLOCK, A_BLOCK), A_BLOCK)
        o_ref[rows, :] = _bf(m_ref[rows, :] * _silu(gate_ref[rows, :].astype(F32)))
        return carry

    lax.fori_loop(0, n_blocks, finish, 0)


def _dilated_attention(proj):
    bsz, s, _ = proj.shape
    hp_blocks = D_MODEL // LANES

    def col_spec(off):
        return pl.BlockSpec((None, s, LANES), lambda b, hp: (b, 0, (off + 1) * hp_blocks + hp))

    return pl.pallas_call(
        functools.partial(_attn_kernel, seq=s),
        grid=(bsz, hp_blocks),
        in_specs=[col_spec(0), col_spec(1), col_spec(2), col_spec(3)],
        out_specs=pl.BlockSpec((None, s, LANES), lambda b, hp: (b, 0, hp)),
        out_shape=jax.ShapeDtypeStruct((bsz, s, D_MODEL), BF16),
        scratch_shapes=[pltpu.VMEM((s, LANES), F32)] * 6
        + [pltpu.VMEM((2 * len(A_DILATIONS), A_BLOCK, 2 * A_BLOCK), F32)],
        compiler_params=_params("parallel", "parallel"),
        name="dilated_attn",
    )(proj, proj, proj, proj)


def _hgrn_kernel(q_ref, logf_ref, k_ref, i_ref, gate_ref, nw_ref, o_ref, st_ref, *, tb):
    @pl.when(pl.program_id(2) == 0)
    def _():
        st_ref[...] = jnp.zeros_like(st_ref)

    n_ch = tb // CHUNK
    r64 = lax.broadcasted_iota(jnp.int32, (CHUNK, CHUNK), 0)
    c64 = lax.broadcasted_iota(jnp.int32, (CHUNK, CHUNK), 1)
    tri = _bf(jnp.where(r64 >= c64, 1.0, 0.0))
    row = lax.broadcasted_iota(jnp.int32, (CHUNK, HEAD), 0)
    sub = lax.broadcasted_iota(jnp.int32, (SUBLANES, HEAD), 0)
    units = [(hd, slice(c * CHUNK, (c + 1) * CHUNK), slice(hd * HEAD, (hd + 1) * HEAD))
             for c in range(n_ch) for hd in range(HGRN_KH)]
    n_units = len(units)
    qs = [q_ref[rows, cols].astype(F32) for _, rows, cols in units]
    ks = [k_ref[rows, cols].astype(F32) for _, rows, cols in units]
    vbs = [i_ref[rows, cols] for _, rows, cols in units]

    gs = []
    for _, rows, cols in units:
        lf = logf_ref[rows, cols]
        hi = _bf(lf)
        r1 = lf - hi.astype(F32)
        mid = _bf(r1)
        lo = _bf(r1 - mid.astype(F32))
        cum = _dot(tri, jnp.concatenate([hi, mid, lo], axis=1))
        gs.append(cum[:, :HEAD] + cum[:, HEAD:2 * HEAD] + cum[:, 2 * HEAD:])

    attn = [None] * n_units
    s = CHUNK // 2
    while s >= 1:
        blk = 2 * s
        upper = (row % blk) >= s
        mask = ((r64 // blk) == (c64 // blk)) & ((r64 % blk) >= s) & ((c64 % blk) < s)
        for c in range(n_units):
            g = gs[c]
            if blk >= SUBLANES:
                g_ref = jnp.concatenate(
                    [jnp.broadcast_to(g[b0 + s - 1:b0 + s, :], (blk, HEAD)) for b0 in range(0, CHUNK, blk)],
                    axis=0)
            elif s == 2:
                g_ref = jnp.concatenate(
                    [jnp.where(sub < 4, jnp.broadcast_to(g[v0 + 1:v0 + 2, :], (SUBLANES, HEAD)),
                               jnp.broadcast_to(g[v0 + 5:v0 + 6, :], (SUBLANES, HEAD)))
                     for v0 in range(0, CHUNK, SUBLANES)], axis=0)
            else:
                g_ref = jnp.where(upper, pltpu.roll(g, 1, 0), g)
            z = _bf(jnp.where(upper, qs[c], ks[c]) * jnp.exp(-jnp.abs(g - g_ref)))
            m = _dot_nt(z, z)
            attn[c] = jnp.where(mask, m, 0.0) if attn[c] is None else jnp.where(mask, m, attn[c])
        s //= 2
    eye = r64 == c64
    attn = [jnp.where(eye, jnp.sum(qs[c] * ks[c], axis=-1, keepdims=True), attn[c]) for c in range(n_units)]

    o_intra = [_dot(_bf(attn[c]), vbs[c]) for c in range(n_units)]
    q_dec = [_bf(qs[c] * jnp.exp(gs[c])) for c in range(n_units)]
    g_last = [gs[c][CHUNK - 1:CHUNK, :] for c in range(n_units)]
    kv = [_dot_tn(vbs[c], _bf(ks[c] * jnp.exp(g_last[c] - gs[c]))) for c in range(n_units)]

    states = [st_ref[hd] for hd in range(HGRN_KH)]
    outs = []
    for c, (hd, _, _) in enumerate(units):
        outs.append(_dot_nt(q_dec[c], _bf(states[hd])) + o_intra[c])
        states[hd] = states[hd] * jnp.exp(g_last[c]) + kv[c]
    for hd in range(HGRN_KH):
        st_ref[hd] = states[hd]

    for c, (_, rows, cols) in enumerate(units):
        o = outs[c]
        ms = jnp.mean(o * o, axis=-1, keepdims=True)
        o = o * lax.rsqrt(ms + RMS_EPS) * nw_ref[...]
        o_ref[rows, cols] = _bf(o * _silu(gate_ref[rows, cols].astype(F32)))


def _hgrn(proj, log_f, k, norm_w, tb):
    bsz, s, _ = proj.shape
    width = HGRN_KH * HEAD
    groups = D_MODEL // width
    base = 5 * groups

    def col_spec(off):
        return pl.BlockSpec((None, tb, width), lambda b, h, t: (b, t, base + off * groups + h))

    head_spec = pl.BlockSpec((None, tb, width), lambda b, h, t: (b, t, h))
    return pl.pallas_call(
        functools.partial(_hgrn_kernel, tb=tb),
        grid=(bsz, groups, s // tb),
        in_specs=[col_spec(0), head_spec, head_spec, col_spec(1), col_spec(2),
                  pl.BlockSpec((1, HEAD), lambda b, h, t: (0, 0))],
        out_specs=head_spec,
        out_shape=jax.ShapeDtypeStruct((bsz, s, D_MODEL), BF16),
        scratch_shapes=[pltpu.VMEM((HGRN_KH, HEAD, HEAD), F32)],
        compiler_params=_params("parallel", "parallel", "arbitrary"),
        name="hgrn2",
    )(proj, log_f, k, proj, proj, norm_w.reshape(1, HEAD))


def _unit_lower_inverses(lows):
    r = lax.broadcasted_iota(jnp.int32, (CHUNK, CHUNK), 0)
    c = lax.broadcasted_iota(jnp.int32, (CHUNK, CHUNK), 1)
    eye = jnp.where(r == c, 1.0, 0.0).astype(F32)
    s = 1
    invs = None
    while s < CHUNK:
        sel = ((r // (2 * s)) == (c // (2 * s))) & ((r % (2 * s)) >= s) & ((c % (2 * s)) < s)
        low_s = [jnp.where(sel, low, 0.0) for low in lows]
        if s == 1:
            invs = [eye - x for x in low_s]
        else:
            inv_b = [_bf(x) for x in invs]
            tmp = [_bf(_dot(a, _bf(x))) for a, x in zip(inv_b, low_s)]
            invs = [x - _dot(t, a) for x, t, a in zip(invs, tmp, inv_b)]
        s *= 2
    return invs


def _gdn_kernel(q_ref, k_ref, v_ref, z_ref, gates_ref, avec_ref, dtb_ref, nw_ref, o_ref, st_ref, *, tb):
    tt = pl.program_id(2)
    group = pl.program_id(1)
    n_ch = tb // CHUNK

    @pl.when(tt == 0)
    def _():
        st_ref[...] = jnp.zeros_like(st_ref)

    q_all = [q_ref[:, kk * HEAD:(kk + 1) * HEAD].astype(F32) for kk in range(GDN_KH)]
    k_all = [k_ref[:, kk * HEAD:(kk + 1) * HEAD].astype(F32) for kk in range(GDN_KH)]

    gl = gates_ref[...]
    beta_all = jax.nn.sigmoid(gl)
    g_all = avec_ref[...] * _softplus(gl + dtb_ref[...])
    lane = lax.broadcasted_iota(jnp.int32, (tb, LANES), 1)

    r64 = lax.broadcasted_iota(jnp.int32, (CHUNK, CHUNK), 0)
    c64 = lax.broadcasted_iota(jnp.int32, (CHUNK, CHUNK), 1)
    tri = _bf(jnp.where(r64 >= c64, 1.0, 0.0))
    causal = r64 >= c64
    strict = r64 > c64
    strict_f = jnp.where(strict, 1.0, 0.0).astype(F32)

    chunk_rows = [slice(c * CHUNK, (c + 1) * CHUNK) for c in range(n_ch)]
    n_vh = 2 * GDN_KH
    units = [(vh, c) for c in range(n_ch) for vh in range(n_vh)]

    beta_cols, g_cols = [], []
    for vh in range(n_vh):
        head = n_vh * group + vh
        beta_cols.append(jnp.sum(jnp.where(lane == head, beta_all, 0.0), axis=-1, keepdims=True))
        g_cols.append(jnp.sum(jnp.where(lane == N_V_HEADS + head, g_all, 0.0), axis=-1, keepdims=True))

    kq = [[_dot_nt(_bf(jnp.concatenate([k_all[kk][rows, :], q_all[kk][rows, :]], axis=0)),
                   _bf(k_all[kk][rows, :])) for rows in chunk_rows]
          for kk in range(GDN_KH)]

    decay, g_i, g_last = [], [], []
    for vh, c in units:
        gb = jnp.broadcast_to(g_cols[vh][chunk_rows[c], :], (CHUNK, CHUNK))
        wmat = jnp.concatenate([gb * strict_f, gb], axis=1)
        hi = _bf(wmat)
        lo = _bf(wmat - hi.astype(F32))
        cum = _dot(tri, hi) + _dot(tri, lo)
        decay.append(jnp.where(causal, jnp.exp(jnp.minimum(cum[:, :CHUNK], 0.0)), 0.0))
        g_i.append(cum[:, CHUNK:CHUNK + 1])
        g_last.append(cum[CHUNK - 1:CHUNK, CHUNK:CHUNK + 1])

    lows = [jnp.where(strict, kq[vh // 2][c][:CHUNK, :] * beta_cols[vh][chunk_rows[c], :] * decay[n], 0.0)
            for n, (vh, c) in enumerate(units)]
    invs = _unit_lower_inverses(lows)

    sols = []
    for n, (vh, c) in enumerate(units):
        rows = chunk_rows[c]
        beta = beta_cols[vh][rows, :]
        v = v_ref[rows, vh * HEAD:(vh + 1) * HEAD].astype(F32)
        rhs = jnp.concatenate([v * beta, k_all[vh // 2][rows, :] * (beta * jnp.exp(g_i[n]))], axis=1)
        sols.append(_bf(_dot(_bf(invs[n]), _bf(rhs))))

    o_loc, q_eff, c_mat, p_mat = [], [], [], []
    for n, (vh, c) in enumerate(units):
        rows = chunk_rows[c]
        attn = _bf(kq[vh // 2][c][CHUNK:, :] * decay[n])
        auw = _dot(attn, sols[n])
        o_loc.append(auw[:, :HEAD])
        q_eff.append(_bf(q_all[vh // 2][rows, :] * jnp.exp(g_i[n]) - auw[:, HEAD:]))
        k_tail = _bf(k_all[vh // 2][rows, :] * jnp.exp(g_last[n] - g_i[n]))
        ktuw = _dot_tn(k_tail, sols[n])
        c_mat.append(ktuw[:, :HEAD])
        p_mat.append(_bf(ktuw[:, HEAD:]))

    states = [st_ref[vh] for vh in range(n_vh)]
    outs = [None] * len(units)
    for n, (vh, c) in enumerate(units):
        sb = _bf(states[vh])
        outs[n] = _dot(q_eff[n], sb) + o_loc[n]
        states[vh] = states[vh] * jnp.exp(g_last[n]) + c_mat[n] - _dot(p_mat[n], sb)
    for vh in range(n_vh):
        st_ref[vh] = states[vh]

    for n, (vh, c) in enumerate(units):
        rows = chunk_rows[c]
        o = outs[n]
        ms = jnp.mean(o * o, axis=-1, keepdims=True)
        o = o * lax.rsqrt(ms + RMS_EPS) * nw_ref[...]
        z = z_ref[rows, vh * HEAD:(vh + 1) * HEAD].astype(F32)
        o_ref[rows, vh * HEAD:(vh + 1) * HEAD] = _bf(o * _silu(z))


def _gdn(qkv, z, gate_logits, a_vec, dtb_vec, norm_w, tb):
    bsz, s, _ = qkv.shape
    width = GDN_KH * HEAD
    groups = D_MODEL // width

    in_specs = [
        pl.BlockSpec((None, tb, width), lambda b, h, t: (b, t, h)),
        pl.BlockSpec((None, tb, width), lambda b, h, t: (b, t, groups + h)),
        pl.BlockSpec((None, tb, 2 * width), lambda b, h, t: (b, t, groups + h)),
        pl.BlockSpec((None, tb, 2 * width), lambda b, h, t: (b, t, h)),
        pl.BlockSpec((None, tb, LANES), lambda b, h, t: (b, t, 0)),
        pl.BlockSpec((1, LANES), lambda b, h, t: (0, 0)),
        pl.BlockSpec((1, LANES), lambda b, h, t: (0, 0)),
        pl.BlockSpec((1, HEAD), lambda b, h, t: (0, 0)),
    ]
    return pl.pallas_call(
        functools.partial(_gdn_kernel, tb=tb),
        grid=(bsz, groups, s // tb),
        in_specs=in_specs,
        out_specs=pl.BlockSpec((None, tb, 2 * width), lambda b, h, t: (b, t, h)),
        out_shape=jax.ShapeDtypeStruct((bsz, s, 2 * D_MODEL), BF16),
        scratch_shapes=[pltpu.VMEM((2 * GDN_KH, HEAD, HEAD), F32)],
        compiler_params=_params("parallel", "parallel", "arbitrary"),
        name="gated_deltanet",
    )(qkv, qkv, qkv, z, gate_logits, a_vec, dtb_vec, norm_w.reshape(1, HEAD))


def kernel(x, norm_w, final_norm_w, even_w_in, even_w_out, hgrn_lb_logits, hgrn_norm_w,
           odd_w_in, odd_conv_w, odd_dt_bias, odd_a_log, odd_norm_w, odd_w_out):
    bsz, s, d = x.shape
    depth = norm_w.shape[0]
    t = bsz * s
    tm = 1024
    tm_in = 1024
    tb = 512

    lb_all = jnp.cumsum(jax.nn.softmax(hgrn_lb_logits.astype(F32), axis=0), axis=0)
    lb_all = jnp.maximum(lb_all - lb_all[0:1], 0.0)
    log_lb = jnp.log(lb_all)
    log_1mlb = jnp.log1p(-lb_all)

    h = x.reshape(t, d)
    for layer in range(depth):
        j = layer // 2
        final_w = final_norm_w if layer == depth - 1 else None
        if layer % 2 == 0:
            proj, log_f, k = _even_proj(h, norm_w[layer], _bf(even_w_in[j]), log_lb[j], log_1mlb[j],
                                        tm_in)
            proj = proj.reshape(bsz, s, EVEN_IN)
            a_mix = _dilated_attention(proj)
            b_mix = _hgrn(proj, log_f.reshape(bsz, s, d), k.reshape(bsz, s, d), hgrn_norm_w[j], tb // 2)
            h = _out_proj(h, [a_mix.reshape(t, d), b_mix.reshape(t, d)], _bf(even_w_out[j]),
                          final_w, tm)
        else:
            w_in = _bf(odd_w_in[j])
            pad = jnp.zeros((d, ODD_REST_PAD - (w_in.shape[1] - ODD_CONV)), BF16)
            w_rest = jnp.concatenate([w_in[:, ODD_CONV:], pad], axis=1)
            qkv = _qkv_conv_proj(h, norm_w[layer], w_in, odd_conv_w[j], tm_in, s)
            z, gate_logits = _norm_matmul(h, norm_w[layer], w_rest, tm_in, ODD_REST_PAD,
                                          side_start=ODD_REST_PAD - LANES, side_width=LANES)
            zeros16 = jnp.zeros((N_V_HEADS,), F32)
            tail = jnp.zeros((LANES - 2 * N_V_HEADS,), F32)
            a_vec = jnp.concatenate([zeros16, -jnp.exp(odd_a_log[j].astype(F32)), tail]).reshape(1, LANES)
            dtb_vec = jnp.concatenate([zeros16, odd_dt_bias[j].astype(F32), tail]).reshape(1, LANES)
            o = _gdn(qkv.reshape(bsz, s, ODD_CONV), z.reshape(bsz, s, ODD_REST_PAD),
                     gate_logits.reshape(bsz, s, LANES), a_vec, dtb_vec, odd_norm_w[j], tb // 4)
            h = _out_proj(h, [o.reshape(t, 2 * d)], _bf(odd_w_out[j]), final_w, tm)
    return h.reshape(bsz, s, d)
```

```python
import functools

import jax
import jax.numpy as jnp
from jax import lax
from jax.experimental import pallas as pl
from jax.experimental.pallas import tpu as pltpu

F32 = jnp.float32
BF16 = jnp.bfloat16

RMS_EPS = 1e-6
LANES = 128
SUBLANES = 8
VMEM_LIMIT = 48 * 1024 * 1024

D_MODEL = 1024
A_HEAD_DIM = 64
A_BLOCK = 128
A_DILATIONS = (1, 4, 16)
CHUNK = 64
A_SCALE = A_HEAD_DIM ** -0.5 * 1.4426950408889634
A_MOD = 16
A_GROUP = 16
A_GROUP_FIRST = 32
HEAD = 128
CONV_K = 4
HGRN_KH = 2
GDN_KH = 8

EVEN_IN = 8 * D_MODEL
ODD_CONV = 4 * D_MODEL
QK_TILES = 2
ODD_REST_PAD = 2 * D_MODEL + LANES
N_V_HEADS = 16


def _dot(a, b):
    return jnp.dot(a, b, preferred_element_type=F32)


def _dot_nt(a, b, precision=None):
    return lax.dot_general(a, b, (((1,), (1,)), ((), ())), precision=precision,
                           preferred_element_type=F32)


def _dot_tn(a, b):
    return lax.dot_general(a, b, (((0,), (0,)), ((), ())), preferred_element_type=F32)


def _bf(x):
    return x.astype(BF16)


def _silu(x):
    return x * jax.nn.sigmoid(x)


def _softplus(x):
    return jnp.maximum(x, 0.0) + jnp.log1p(jnp.exp(-jnp.abs(x)))


def _params(*sem):
    return pltpu.CompilerParams(dimension_semantics=sem, vmem_limit_bytes=VMEM_LIMIT)


def _norm_rows(x_ref, nw_ref, xn_ref):
    x = x_ref[...]
    ms = jnp.mean(x * x, axis=-1, keepdims=True)
    xn_ref[...] = _bf(x * lax.rsqrt(ms + RMS_EPS) * nw_ref[...])


def _norm_matmul_kernel(x_ref, nw_ref, w_ref, o_ref, side_ref, xn_ref, *, side_tile, side_lo):
    j = pl.program_id(1)
    pl.when(j == 0)(functools.partial(_norm_rows, x_ref, nw_ref, xn_ref))
    acc = _dot(xn_ref[...], w_ref[...])
    o_ref[...] = _bf(acc)

    @pl.when(j == side_tile)
    def _():
        side_ref[...] = acc[:, side_lo:side_lo + side_ref.shape[1]]


def _norm_matmul(x, nw, w, tm, tn, side_start, side_width):
    t, d = x.shape
    n = w.shape[1]
    side_tile, side_lo = divmod(side_start, tn)
    assert side_lo + side_width <= tn
    return pl.pallas_call(
        functools.partial(_norm_matmul_kernel, side_tile=side_tile, side_lo=side_lo),
        grid=(t // tm, n // tn),
        in_specs=[pl.BlockSpec((tm, d), lambda i, j: (i, 0)),
                  pl.BlockSpec((1, d), lambda i, j: (0, 0)),
                  pl.BlockSpec((d, tn), lambda i, j: (0, j))],
        out_specs=[pl.BlockSpec((tm, tn), lambda i, j: (i, j)),
                   pl.BlockSpec((tm, side_width), lambda i, j: (i, 0))],
        out_shape=[jax.ShapeDtypeStruct((t, n), BF16),
                   jax.ShapeDtypeStruct((t, side_width), F32)],
        scratch_shapes=[pltpu.VMEM((tm, d), BF16)],
        compiler_params=_params("parallel", "arbitrary"),
        name="norm_in_proj",
    )(x, nw.reshape(1, d), w)


def _even_proj_kernel(x_ref, nw_ref, w_ref, loglb_ref, log1mlb_ref, o_ref, logf_ref, k_ref, xn_ref,
                      fbuf, *, n_tiles):
    j = pl.program_id(1)
    pl.when(j == 0)(functools.partial(_norm_rows, x_ref, nw_ref, xn_ref))
    acc = _dot(xn_ref[...], w_ref[...])
    o_ref[...] = _bf(acc)

    @pl.when(j == 0)
    def _():
        fbuf[...] = acc

    step_rows = fbuf.shape[0] // n_tiles
    rows = pl.ds(pl.multiple_of(j * step_rows, step_rows), step_rows)
    x = fbuf[rows, :]
    log_sig = jnp.minimum(x, 0.0) - jnp.log1p(jnp.exp(-jnp.abs(x)))
    log_lb = loglb_ref[...]
    b = log1mlb_ref[...] + log_sig
    logf_ref[rows, :] = jnp.maximum(log_lb, b) + jnp.log1p(jnp.exp(-jnp.abs(log_lb - b)))
    k_ref[rows, :] = _bf(jnp.exp(b - x))


def _even_proj(x, nw, w, log_lb, log_1mlb, tm):
    t, d = x.shape
    n = w.shape[1]
    n_tiles = n // d
    f_tile = 5

    def w_tile(i, j):
        return 0, jnp.where(j == 0, f_tile, jnp.where(j <= f_tile, j - 1, j))

    return pl.pallas_call(
        functools.partial(_even_proj_kernel, n_tiles=n_tiles),
        grid=(t // tm, n_tiles),
        in_specs=[pl.BlockSpec((tm, d), lambda i, j: (i, 0)),
                  pl.BlockSpec((1, d), lambda i, j: (0, 0)),
                  pl.BlockSpec((d, d), w_tile),
                  pl.BlockSpec((1, d), lambda i, j: (0, 0)),
                  pl.BlockSpec((1, d), lambda i, j: (0, 0))],
        out_specs=[pl.BlockSpec((tm, d), lambda i, j: (i, j)),
                   pl.BlockSpec((tm, d), lambda i, j: (i, 0)),
                   pl.BlockSpec((tm, d), lambda i, j: (i, 0))],
        out_shape=[jax.ShapeDtypeStruct((t, n), BF16),
                   jax.ShapeDtypeStruct((t, d), F32),
                   jax.ShapeDtypeStruct((t, d), BF16)],
        scratch_shapes=[pltpu.VMEM((tm, d), BF16), pltpu.VMEM((tm, d), F32)],
        compiler_params=_params("parallel", "arbitrary"),
        name="even_in_proj",
    )(x, nw.reshape(1, d), w, log_lb.reshape(1, d), log_1mlb.reshape(1, d))


def _qkv_conv_kernel(x_ref, nw_ref, w_ref, cw_ref, o_ref, xn_ref, halo_ref, *, tiles_per_seq):
    i = pl.program_id(0)
    j = pl.program_id(1)
    tm, tn = o_ref.shape
    halo = halo_ref.shape[1]

    pl.when(j == 0)(functools.partial(_norm_rows, x_ref, nw_ref, xn_ref))

    @pl.when(i == 0)
    def _():
        halo_ref[j] = jnp.zeros((halo, tn), F32)

    acc = _dot(xn_ref[...], w_ref[...])
    prev = jnp.where(i % tiles_per_seq == 0, 0.0, halo_ref[j])
    halo_ref[j] = acc[tm - halo:, :]
    ext = jnp.concatenate([prev, acc], axis=0)
    cw = cw_ref[...]
    conv = ext * cw[0:1, :]
    for k in range(1, CONV_K):
        conv = ext * cw[k:k + 1, :] + pltpu.roll(conv, 1, 0)
    y = _silu(conv[halo:, :])

    @pl.when(j < QK_TILES)
    def _():
        scale = jnp.where(j == 0, HEAD ** -0.5, 1.0)
        for h0 in range(0, tn, HEAD):
            yh = y[:, h0:h0 + HEAD]
            inv = lax.rsqrt(jnp.sum(yh * yh, axis=-1, keepdims=True) + RMS_EPS) * scale
            o_ref[:, h0:h0 + HEAD] = _bf(yh * inv)

    @pl.when(j >= QK_TILES)
    def _():
        o_ref[...] = _bf(y)


def _qkv_conv_proj(x, nw, w, conv_w, tm, seq):
    t, d = x.shape
    n = conv_w.shape[1]
    tn = d
    return pl.pallas_call(
        functools.partial(_qkv_conv_kernel, tiles_per_seq=seq // tm),
        grid=(t // tm, n // tn),
        in_specs=[pl.BlockSpec((tm, d), lambda i, j: (i, 0)),
                  pl.BlockSpec((1, d), lambda i, j: (0, 0)),
                  pl.BlockSpec((d, tn), lambda i, j: (0, j)),
                  pl.BlockSpec((CONV_K, tn), lambda i, j: (0, j))],
        out_specs=pl.BlockSpec((tm, tn), lambda i, j: (i, j)),
        out_shape=jax.ShapeDtypeStruct((t, n), BF16),
        scratch_shapes=[pltpu.VMEM((tm, d), BF16),
                        pltpu.VMEM((n // tn, SUBLANES, tn), F32)],
        compiler_params=_params("arbitrary", "arbitrary"),
        name="qkv_conv_proj",
    )(x, nw.reshape(1, d), w, conv_w)


def _out_proj_kernel(*refs, n_in, final):
    h_ref = refs[0]
    a_refs = refs[1:1 + n_in]
    w_refs = refs[1 + n_in:1 + 2 * n_in]
    o_ref = refs[-1]
    acc = h_ref[...]
    for a_ref, w_ref in zip(a_refs, w_refs):
        acc = acc + _dot(a_ref[...], w_ref[...])
    if final:
        fw_ref = refs[1 + 2 * n_in]
        ms = jnp.mean(acc * acc, axis=-1, keepdims=True)
        acc = acc * lax.rsqrt(ms + RMS_EPS) * fw_ref[...]
    o_ref[...] = acc


def _out_proj(h, acts, w, final_w, tm):
    t, d = h.shape
    n_in = len(acts)
    k = acts[0].shape[1]
    final = final_w is not None
    in_specs = [pl.BlockSpec((tm, d), lambda i: (i, 0))]
    in_specs += [pl.BlockSpec((tm, k), lambda i: (i, 0)) for _ in acts]
    in_specs += [pl.BlockSpec((k, d), lambda i, rb=rb: (rb, 0)) for rb in range(n_in)]
    args = [h, *acts] + [w] * n_in
    if final:
        in_specs.append(pl.BlockSpec((1, d), lambda i: (0, 0)))
        args.append(final_w.reshape(1, d))
    return pl.pallas_call(
        functools.partial(_out_proj_kernel, n_in=n_in, final=final),
        grid=(t // tm,),
        in_specs=in_specs,
        out_specs=pl.BlockSpec((tm, d), lambda i: (i, 0)),
        out_shape=jax.ShapeDtypeStruct((t, d), F32),
        compiler_params=_params("parallel"),
        name="out_proj",
    )(*args)


def _attn_kernel(q_ref, k_ref, v_ref, gate_ref, o_ref, qp_ref, kp_ref, vp_ref, acc_ref, m_ref, l_ref,
                 bias_ref, *, seq):
    per_res = seq // A_MOD
    n_blocks = seq // A_BLOCK
    row = lax.broadcasted_iota(jnp.int32, (A_BLOCK, 2 * A_BLOCK), 0)
    col = lax.broadcasted_iota(jnp.int32, (A_BLOCK, 2 * A_BLOCK), 1)
    lane = lax.broadcasted_iota(jnp.int32, (A_BLOCK, LANES), 1)
    head0 = lane < A_HEAD_DIM

    def widen(t, carry):
        rows = pl.ds(pl.multiple_of(t * A_BLOCK, A_BLOCK), A_BLOCK)
        acc_ref[rows, :] = q_ref[rows, :].astype(F32) * A_SCALE
        m_ref[rows, :] = k_ref[rows, :].astype(F32)
        l_ref[rows, :] = v_ref[rows, :].astype(F32)
        return carry

    lax.fori_loop(0, n_blocks, widen, 0)

    def to_residue_major(res, carry):
        src = pl.ds(res, per_res, stride=A_MOD)
        dst = pl.ds(pl.multiple_of(res * per_res, per_res), per_res)
        qp_ref[dst, :] = acc_ref[src, :]
        kp_ref[dst, :] = m_ref[src, :]
        vp_ref[dst, :] = l_ref[src, :]
        return carry

    lax.fori_loop(0, A_MOD, to_residue_major, 0)

    for dilation in A_DILATIONS:
        first = dilation == A_DILATIONS[0]
        per_class = n_blocks // dilation
        runs = A_MOD // dilation
        rq = A_BLOCK // runs
        dist = runs * (row % rq - col % (2 * rq)) + (row // rq - col // (2 * rq)) + A_BLOCK
        band = (dist >= 0) & (dist <= A_BLOCK)
        in_cur = col % (2 * rq) >= rq
        slot = 2 * A_DILATIONS.index(dilation)
        bias_ref[slot] = jnp.where(band & in_cur, 0.0, -jnp.inf)
        bias_ref[slot + 1] = jnp.where(band, 0.0, -jnp.inf)

        def load(n, dilation=dilation, first=first, per_class=per_class, runs=runs, rq=rq):
            res = n // per_class
            i = n % per_class
            cur, prev = [], []
            for c in range(runs):
                base = pl.multiple_of((c * dilation + res) * per_res + rq * i, SUBLANES)
                cur.append(pl.ds(base, rq))
                prev.append(pl.ds(pl.multiple_of(jnp.where(i > 0, base - rq, base), SUBLANES), rq))

            def window(ref):
                return _bf(jnp.concatenate([ref[rows, :] for pc in zip(prev, cur) for rows in pc], axis=0))

            def block(ref):
                return jnp.concatenate([ref[rows, :] for rows in cur], axis=0)

            return cur, i, block(qp_ref), window(kp_ref), window(vp_ref)

        def scores(i, q, kw, vw, slot=slot):
            bias = bias_ref[slot + jnp.where(i > 0, 1, 0)]
            parts = []
            for h in range(2):
                qh = _bf(jnp.where(head0 if h == 0 else ~head0, q, 0.0))
                s = _dot_nt(qh, kw) + bias
                m_blk = jnp.max(s, axis=-1, keepdims=True)
                p = jnp.exp2(s - m_blk)
                parts.append((_dot(_bf(p), vw), m_blk, jnp.sum(p, axis=-1, keepdims=True)))
            return tuple(jnp.where(head0, a, b) for a, b in zip(*parts))

        n_group = min(A_GROUP_FIRST if first else A_GROUP, n_blocks)

        def group(t, carry, load=load, scores=scores, first=first, rq=rq, n_group=n_group):
            loaded = [load(t * n_group + u) for u in range(n_group)]
            results = [scores(*item[1:]) for item in loaded]
            for (cur, *_), (acc, m_blk, l_blk) in zip(loaded, results):
                if not first:
                    acc_old, m_old, l_old = (
                        jnp.concatenate([ref[rows, :] for rows in cur], axis=0)
                        for ref in (acc_ref, m_ref, l_ref))
                    m_new = jnp.maximum(m_old, m_blk)
                    alpha = jnp.exp2(m_old - m_new)
                    beta = jnp.exp2(m_blk - m_new)
                    acc = acc_old * alpha + acc * beta
                    l_blk = l_old * alpha + l_blk * beta
                    m_blk = m_new
                for ref, val in zip((acc_ref, m_ref, l_ref), (acc, m_blk, l_blk)):
                    for c, rows in enumerate(cur):
                        ref[rows, :] = val[c * rq:(c + 1) * rq, :]
            return carry

        lax.fori_loop(0, n_blocks // n_group, group, 0)

    def to_position_order(res, carry):
        src = pl.ds(pl.multiple_of(res * per_res, per_res), per_res)
        m_ref[pl.ds(res, per_res, stride=A_MOD), :] = acc_ref[src, :] / l_ref[src, :]
        return carry

    lax.fori_loop(0, A_MOD, to_position_order, 0)

    def finish(t, carry):
        rows = pl.ds(pl.multiple_of(t * A_BLOCK, A_BLOCK), A_BLOCK)
        o_ref[rows, :] = _bf(m_ref[rows, :] * _silu(gate_ref[rows, :].astype(F32)))
        return carry

    lax.fori_loop(0, n_blocks, finish, 0)


def _dilated_attention(proj):
    bsz, s, _ = proj.shape
    hp_blocks = D_MODEL // LANES

    def col_spec(off):
        return pl.BlockSpec((None, s, LANES), lambda b, hp: (b, 0, (off + 1) * hp_blocks + hp))

    return pl.pallas_call(
        functools.partial(_attn_kernel, seq=s),
        grid=(bsz, hp_blocks),
        in_specs=[col_spec(0), col_spec(1), col_spec(2), col_spec(3)],
        out_specs=pl.BlockSpec((None, s, LANES), lambda b, hp: (b, 0, hp)),
        out_shape=jax.ShapeDtypeStruct((bsz, s, D_MODEL), BF16),
        scratch_shapes=[pltpu.VMEM((s, LANES), F32)] * 6
        + [pltpu.VMEM((2 * len(A_DILATIONS), A_BLOCK, 2 * A_BLOCK), F32)],
        compiler_params=_params("parallel", "parallel"),
        name="dilated_attn",
    )(proj, proj, proj, proj)


def _hgrn_kernel(q_ref, logf_ref, k_ref, i_ref, gate_ref, nw_ref, o_ref, st_ref, *, tb):
    @pl.when(pl.program_id(2) == 0)
    def _():
        st_ref[...] = jnp.zeros_like(st_ref)

    n_ch = tb // CHUNK
    r64 = lax.broadcasted_iota(jnp.int32, (CHUNK, CHUNK), 0)
    c64 = lax.broadcasted_iota(jnp.int32, (CHUNK, CHUNK), 1)
    tri = _bf(jnp.where(r64 >= c64, 1.0, 0.0))
    row = lax.broadcasted_iota(jnp.int32, (CHUNK, HEAD), 0)
    sub = lax.broadcasted_iota(jnp.int32, (SUBLANES, HEAD), 0)
    units = [(hd, slice(c * CHUNK, (c + 1) * CHUNK), slice(hd * HEAD, (hd + 1) * HEAD))
             for c in range(n_ch) for hd in range(HGRN_KH)]
    n_units = len(units)
    qs = [q_ref[rows, cols].astype(F32) for _, rows, cols in units]
    ks = [k_ref[rows, cols].astype(F32) for _, rows, cols in units]
    vbs = [i_ref[rows, cols] for _, rows, cols in units]

    gs = []
    for _, rows, cols in units:
        lf = logf_ref[rows, cols]
        hi = _bf(lf)
        r1 = lf - hi.astype(F32)
        mid = _bf(r1)
        lo = _bf(r1 - mid.astype(F32))
        cum = _dot(tri, jnp.concatenate([hi, mid, lo], axis=1))
        gs.append(cum[:, :HEAD] + cum[:, HEAD:2 * HEAD] + cum[:, 2 * HEAD:])

    attn = [None] * n_units
    s = CHUNK // 2
    while s >= 1:
        blk = 2 * s
        upper = (row % blk) >= s
        mask = ((r64 // blk) == (c64 // blk)) & ((r64 % blk) >= s) & ((c64 % blk) < s)
        for c in range(n_units):
            g = gs[c]
            if blk >= SUBLANES:
                g_ref = jnp.concatenate(
                    [jnp.broadcast_to(g[b0 + s - 1:b0 + s, :], (blk, HEAD)) for b0 in range(0, CHUNK, blk)],
                    axis=0)
            elif s == 2:
                g_ref = jnp.concatenate(
                    [jnp.where(sub < 4, jnp.broadcast_to(g[v0 + 1:v0 + 2, :], (SUBLANES, HEAD)),
                               jnp.broadcast_to(g[v0 + 5:v0 + 6, :], (SUBLANES, HEAD)))
                     for v0 in range(0, CHUNK, SUBLANES)], axis=0)
            else:
                g_ref = jnp.where(upper, pltpu.roll(g, 1, 0), g)
            z = _bf(jnp.where(upper, qs[c], ks[c]) * jnp.exp(-jnp.abs(g - g_ref)))
            m = _dot_nt(z, z)
            attn[c] = jnp.where(mask, m, 0.0) if attn[c] is None else jnp.where(mask, m, attn[c])
        s //= 2
    eye = r64 == c64
    attn = [jnp.where(eye, jnp.sum(qs[c] * ks[c], axis=-1, keepdims=True), attn[c]) for c in range(n_units)]

    o_intra = [_dot(_bf(attn[c]), vbs[c]) for c in range(n_units)]
    q_dec = [_bf(qs[c] * jnp.exp(gs[c])) for c in range(n_units)]
    g_last = [gs[c][CHUNK - 1:CHUNK, :] for c in range(n_units)]
    kv = [_dot_tn(vbs[c], _bf(ks[c] * jnp.exp(g_last[c] - gs[c]))) for c in range(n_units)]

    states = [st_ref[hd] for hd in range(HGRN_KH)]
    outs = []
    for c, (hd, _, _) in enumerate(units):
        outs.append(_dot_nt(q_dec[c], _bf(states[hd])) + o_intra[c])
        states[hd] = states[hd] * jnp.exp(g_last[c]) + kv[c]
    for hd in range(HGRN_KH):
        st_ref[hd] = states[hd]

    for c, (_, rows, cols) in enumerate(units):
        o = outs[c]
        ms = jnp.mean(o * o, axis=-1, keepdims=True)
        o = o * lax.rsqrt(ms + RMS_EPS) * nw_ref[...]
        o_ref[rows, cols] = _bf(o * _silu(gate_ref[rows, cols].astype(F32)))


def _hgrn(proj, log_f, k, norm_w, tb):
    bsz, s, _ = proj.shape
    width = HGRN_KH * HEAD
    groups = D_MODEL // width
    base = 5 * groups

    def col_spec(off):
        return pl.BlockSpec((None, tb, width), lambda b, h, t: (b, t, base + off * groups + h))

    head_spec = pl.BlockSpec((None, tb, width), lambda b, h, t: (b, t, h))
    return pl.pallas_call(
        functools.partial(_hgrn_kernel, tb=tb),
        grid=(bsz, groups, s // tb),
        in_specs=[col_spec(0), head_spec, head_spec, col_spec(1), col_spec(2),
                  pl.BlockSpec((1, HEAD), lambda b, h, t: (0, 0))],
        out_specs=head_spec,
        out_shape=jax.ShapeDtypeStruct((bsz, s, D_MODEL), BF16),
        scratch_shapes=[pltpu.VMEM((HGRN_KH, HEAD, HEAD), F32)],
        compiler_params=_params("parallel", "parallel", "arbitrary"),
        name="hgrn2",
    )(proj, log_f, k, proj, proj, norm_w.reshape(1, HEAD))


def _unit_lower_inverses(lows):
    r = lax.broadcasted_iota(jnp.int32, (CHUNK, CHUNK), 0)
    c = lax.broadcasted_iota(jnp.int32, (CHUNK, CHUNK), 1)
    eye = jnp.where(r == c, 1.0, 0.0).astype(F32)
    s = 1
    invs = None
    while s < CHUNK:
        sel = ((r // (2 * s)) == (c // (2 * s))) & ((r % (2 * s)) >= s) & ((c % (2 * s)) < s)
        low_s = [jnp.where(sel, low, 0.0) for low in lows]
        if s == 1:
            invs = [eye - x for x in low_s]
        else:
            inv_b = [_bf(x) for x in invs]
            tmp = [_bf(_dot(a, _bf(x))) for a, x in zip(inv_b, low_s)]
            invs = [x - _dot(t, a) for x, t, a in zip(invs, tmp, inv_b)]
        s *= 2
    return invs


def _gdn_kernel(q_ref, k_ref, v_ref, z_ref, gates_ref, avec_ref, dtb_ref, nw_ref, o_ref, st_ref, *, tb):
    tt = pl.program_id(2)
    group = pl.program_id(1)
    n_ch = tb // CHUNK

    @pl.when(tt == 0)
    def _():
        st_ref[...] = jnp.zeros_like(st_ref)

    q_all = [q_ref[:, kk * HEAD:(kk + 1) * HEAD].astype(F32) for kk in range(GDN_KH)]
    k_all = [k_ref[:, kk * HEAD:(kk + 1) * HEAD].astype(F32) for kk in range(GDN_KH)]

    gl = gates_ref[...]
    beta_all = jax.nn.sigmoid(gl)
    g_all = avec_ref[...] * _softplus(gl + dtb_ref[...])
    lane = lax.broadcasted_iota(jnp.int32, (tb, LANES), 1)

    r64 = lax.broadcasted_iota(jnp.int32, (CHUNK, CHUNK), 0)
    c64 = lax.broadcasted_iota(jnp.int32, (CHUNK, CHUNK), 1)
    tri = _bf(jnp.where(r64 >= c64, 1.0, 0.0))
    causal = r64 >= c64
    strict = r64 > c64
    strict_f = jnp.where(strict, 1.0, 0.0).astype(F32)

    chunk_rows = [slice(c * CHUNK, (c + 1) * CHUNK) for c in range(n_ch)]
    n_vh = 2 * GDN_KH
    units = [(vh, c) for c in range(n_ch) for vh in range(n_vh)]

    beta_cols, g_cols = [], []
    for vh in range(n_vh):
        head = n_vh * group + vh
        beta_cols.append(jnp.sum(jnp.where(lane == head, beta_all, 0.0), axis=-1, keepdims=True))
        g_cols.append(jnp.sum(jnp.where(lane == N_V_HEADS + head, g_all, 0.0), axis=-1, keepdims=True))

    kq = [[_dot_nt(_bf(jnp.concatenate([k_all[kk][rows, :], q_all[kk][rows, :]], axis=0)),
                   _bf(k_all[kk][rows, :])) for rows in chunk_rows]
          for kk in range(GDN_KH)]

    decay, g_i, g_last = [], [], []
    for vh, c in units:
        gb = jnp.broadcast_to(g_cols[vh][chunk_rows[c], :], (CHUNK, CHUNK))
        wmat = jnp.concatenate([gb * strict_f, gb], axis=1)
        hi = _bf(wmat)
        lo = _bf(wmat - hi.astype(F32))
        cum = _dot(tri, hi) + _dot(tri, lo)
        decay.append(jnp.where(causal, jnp.exp(jnp.minimum(cum[:, :CHUNK], 0.0)), 0.0))
        g_i.append(cum[:, CHUNK:CHUNK + 1])
        g_last.append(cum[CHUNK - 1:CHUNK, CHUNK:CHUNK + 1])

    lows = [jnp.where(strict, kq[vh // 2][c][:CHUNK, :] * beta_cols[vh][chunk_rows[c], :] * decay[n], 0.0)
            for n, (vh, c) in enumerate(units)]
    invs = _unit_lower_inverses(lows)

    sols = []
    for n, (vh, c) in enumerate(units):
        rows = chunk_rows[c]
        beta = beta_cols[vh][rows, :]
        v = v_ref[rows, vh * HEAD:(vh + 1) * HEAD].astype(F32)
        rhs = jnp.concatenate([v * beta, k_all[vh // 2][rows, :] * (beta * jnp.exp(g_i[n]))], axis=1)
        sols.append(_bf(_dot(_bf(invs[n]), _bf(rhs))))

    o_loc, q_eff, c_mat, p_mat = [], [], [], []
    for n, (vh, c) in enumerate(units):
        rows = chunk_rows[c]
        attn = _bf(kq[vh // 2][c][CHUNK:, :] * decay[n])
        auw = _dot(attn, sols[n])
        o_loc.append(auw[:, :HEAD])
        q_eff.append(_bf(q_all[vh // 2][rows, :] * jnp.exp(g_i[n]) - auw[:, HEAD:]))
        k_tail = _bf(k_all[vh // 2][rows, :] * jnp.exp(g_last[n] - g_i[n]))
        ktuw = _dot_tn(k_tail, sols[n])
        c_mat.append(ktuw[:, :HEAD])
        p_mat.append(_bf(ktuw[:, HEAD:]))

    states = [st_ref[vh] for vh in range(n_vh)]
    outs = [None] * len(units)
    for n, (vh, c) in enumerate(units):
        sb = _bf(states[vh])
        outs[n] = _dot(q_eff[n], sb) + o_loc[n]
        states[vh] = states[vh] * jnp.exp(g_last[n]) + c_mat[n] - _dot(p_mat[n], sb)
    for vh in range(n_vh):
        st_ref[vh] = states[vh]

    for n, (vh, c) in enumerate(units):
        rows = chunk_rows[c]
        o = outs[n]
        ms = jnp.mean(o * o, axis=-1, keepdims=True)
        o = o * lax.rsqrt(ms + RMS_EPS) * nw_ref[...]
        z = z_ref[rows, vh * HEAD:(vh + 1) * HEAD].astype(F32)
        o_ref[rows, vh * HEAD:(vh + 1) * HEAD] = _bf(o * _silu(z))


def _gdn(qkv, z, gate_logits, a_vec, dtb_vec, norm_w, tb):
    bsz, s, _ = qkv.shape
    width = GDN_KH * HEAD
    groups = D_MODEL // width

    in_specs = [
        pl.BlockSpec((None, tb, width), lambda b, h, t: (b, t, h)),
        pl.BlockSpec((None, tb, width), lambda b, h, t: (b, t, groups + h)),
        pl.BlockSpec((None, tb, 2 * width), lambda b, h, t: (b, t, groups + h)),
        pl.BlockSpec((None, tb, 2 * width), lambda b, h, t: (b, t, h)),
        pl.BlockSpec((None, tb, LANES), lambda b, h, t: (b, t, 0)),
        pl.BlockSpec((1, LANES), lambda b, h, t: (0, 0)),
        pl.BlockSpec((1, LANES), lambda b, h, t: (0, 0)),
        pl.BlockSpec((1, HEAD), lambda b, h, t: (0, 0)),
    ]
    return pl.pallas_call(
        functools.partial(_gdn_kernel, tb=tb),
        grid=(bsz, groups, s // tb),
        in_specs=in_specs,
        out_specs=pl.BlockSpec((None, tb, 2 * width), lambda b, h, t: (b, t, h)),
        out_shape=jax.ShapeDtypeStruct((bsz, s, 2 * D_MODEL), BF16),
        scratch_shapes=[pltpu.VMEM((2 * GDN_KH, HEAD, HEAD), F32)],
        compiler_params=_params("parallel", "parallel", "arbitrary"),
        name="gated_deltanet",
    )(qkv, qkv, qkv, z, gate_logits, a_vec, dtb_vec, norm_w.reshape(1, HEAD))


def kernel(x, norm_w, final_norm_w, even_w_in, even_w_out, hgrn_lb_logits, hgrn_norm_w,
           odd_w_in, odd_conv_w, odd_dt_bias, odd_a_log, odd_norm_w, odd_w_out):
    bsz, s, d = x.shape
    depth = norm_w.shape[0]
    t = bsz * s
    tm = 1024
    tm_in = 1024
    tb = 512

    lb_all = jnp.cumsum(jax.nn.softmax(hgrn_lb_logits.astype(F32), axis=0), axis=0)
    lb_all = jnp.maximum(lb_all - lb_all[0:1], 0.0)
    log_lb = jnp.log(lb_all)
    log_1mlb = jnp.log1p(-lb_all)

    h = x.reshape(t, d)
    for layer in range(depth):
        j = layer // 2
        final_w = final_norm_w if layer == depth - 1 else None
        if layer % 2 == 0:
            proj, log_f, k = _even_proj(h, norm_w[layer], _bf(even_w_in[j]), log_lb[j], log_1mlb[j],
                                        tm_in)
            proj = proj.reshape(bsz, s, EVEN_IN)
            a_mix = _dilated_attention(proj)
            b_mix = _hgrn(proj, log_f.reshape(bsz, s, d), k.reshape(bsz, s, d), hgrn_norm_w[j], 2 * tb)
            h = _out_proj(h, [a_mix.reshape(t, d), b_mix.reshape(t, d)], _bf(even_w_out[j]),
                          final_w, tm)
        else:
            w_in = _bf(odd_w_in[j])
            pad = jnp.zeros((d, ODD_REST_PAD - (w_in.shape[1] - ODD_CONV)), BF16)
            w_rest = jnp.concatenate([w_in[:, ODD_CONV:], pad], axis=1)
            qkv = _qkv_conv_proj(h, norm_w[layer], w_in, odd_conv_w[j], tm_in, s)
            z, gate_logits = _norm_matmul(h, norm_w[layer], w_rest, tm_in, ODD_REST_PAD,
                                          side_start=ODD_REST_PAD - LANES, side_width=LANES)
            zeros16 = jnp.zeros((N_V_HEADS,), F32)
            tail = jnp.zeros((LANES - 2 * N_V_HEADS,), F32)
            a_vec = jnp.concatenate([zeros16, -jnp.exp(odd_a_log[j].astype(F32)), tail]).reshape(1, LANES)
            dtb_vec = jnp.concatenate([zeros16, odd_dt_bias[j].astype(F32), tail]).reshape(1, LANES)
            o = _gdn(qkv.reshape(bsz, s, ODD_CONV), z.reshape(bsz, s, ODD_REST_PAD),
                     gate_logits.reshape(bsz, s, LANES), a_vec, dtb_vec, odd_norm_w[j], tb // 4)
            h = _out_proj(h, [o.reshape(t, 2 * d)], _bf(odd_w_out[j]), final_w, tm)
    return h.reshape(bsz, s, d)
```

```python
import functools

import jax
import jax.numpy as jnp
from jax import lax
from jax.experimental import pallas as pl
from jax.experimental.pallas import tpu as pltpu

F32 = jnp.float32
BF16 = jnp.bfloat16

RMS_EPS = 1e-6
LANES = 128
SUBLANES = 8
VMEM_LIMIT = 48 * 1024 * 1024

D_MODEL = 1024
A_HEAD_DIM = 64
A_BLOCK = 128
A_DILATIONS = (1, 4, 16)
CHUNK = 64
A_SCALE = A_HEAD_DIM ** -0.5 * 1.4426950408889634
A_MOD = 16
A_GROUP = 32
A_GROUP_FIRST = 32
HEAD = 128
CONV_K = 4
HGRN_KH = 2
GDN_KH = 8

EVEN_IN = 8 * D_MODEL
ODD_CONV = 4 * D_MODEL
QK_TILES = 2
ODD_REST_PAD = 2 * D_MODEL + LANES
N_V_HEADS = 16


def _dot(a, b):
    return jnp.dot(a, b, preferred_element_type=F32)


def _dot_nt(a, b, precision=None):
    return lax.dot_general(a, b, (((1,), (1,)), ((), ())), precision=precision,
                           preferred_element_type=F32)


def _dot_tn(a, b):
    return lax.dot_general(a, b, (((0,), (0,)), ((), ())), preferred_element_type=F32)


def _bf(x):
    return x.astype(BF16)


def _silu(x):
    return x * jax.nn.sigmoid(x)


def _softplus(x):
    return jnp.maximum(x, 0.0) + jnp.log1p(jnp.exp(-jnp.abs(x)))


def _params(*sem):
    return pltpu.CompilerParams(dimension_semantics=sem, vmem_limit_bytes=VMEM_LIMIT)


def _norm_rows(x_ref, nw_ref, xn_ref):
    x = x_ref[...]
    ms = jnp.mean(x * x, axis=-1, keepdims=True)
    xn_ref[...] = _bf(x * lax.rsqrt(ms + RMS_EPS) * nw_ref[...])


def _norm_matmul_kernel(x_ref, nw_ref, w_ref, o_ref, side_ref, xn_ref, *, side_tile, side_lo):
    j = pl.program_id(1)
    pl.when(j == 0)(functools.partial(_norm_rows, x_ref, nw_ref, xn_ref))
    acc = _dot(xn_ref[...], w_ref[...])
    o_ref[...] = _bf(acc)

    @pl.when(j == side_tile)
    def _():
        side_ref[...] = acc[:, side_lo:side_lo + side_ref.shape[1]]


def _norm_matmul(x, nw, w, tm, tn, side_start, side_width):
    t, d = x.shape
    n = w.shape[1]
    side_tile, side_lo = divmod(side_start, tn)
    assert side_lo + side_width <= tn
    return pl.pallas_call(
        functools.partial(_norm_matmul_kernel, side_tile=side_tile, side_lo=side_lo),
        grid=(t // tm, n // tn),
        in_specs=[pl.BlockSpec((tm, d), lambda i, j: (i, 0)),
                  pl.BlockSpec((1, d), lambda i, j: (0, 0)),
                  pl.BlockSpec((d, tn), lambda i, j: (0, j))],
        out_specs=[pl.BlockSpec((tm, tn), lambda i, j: (i, j)),
                   pl.BlockSpec((tm, side_width), lambda i, j: (i, 0))],
        out_shape=[jax.ShapeDtypeStruct((t, n), BF16),
                   jax.ShapeDtypeStruct((t, side_width), F32)],
        scratch_shapes=[pltpu.VMEM((tm, d), BF16)],
        compiler_params=_params("parallel", "arbitrary"),
        name="norm_in_proj",
    )(x, nw.reshape(1, d), w)


def _even_proj_kernel(x_ref, nw_ref, w_ref, loglb_ref, log1mlb_ref, o_ref, logf_ref, k_ref, xn_ref,
                      fbuf, *, n_tiles):
    j = pl.program_id(1)
    pl.when(j == 0)(functools.partial(_norm_rows, x_ref, nw_ref, xn_ref))
    acc = _dot(xn_ref[...], w_ref[...])
    o_ref[...] = _bf(acc)

    @pl.when(j == 0)
    def _():
        fbuf[...] = acc

    step_rows = fbuf.shape[0] // n_tiles
    rows = pl.ds(pl.multiple_of(j * step_rows, step_rows), step_rows)
    x = fbuf[rows, :]
    log_sig = jnp.minimum(x, 0.0) - jnp.log1p(jnp.exp(-jnp.abs(x)))
    log_lb = loglb_ref[...]
    b = log1mlb_ref[...] + log_sig
    logf_ref[rows, :] = jnp.maximum(log_lb, b) + jnp.log1p(jnp.exp(-jnp.abs(log_lb - b)))
    k_ref[rows, :] = _bf(jnp.exp(b - x))


def _even_proj(x, nw, w, log_lb, log_1mlb, tm):
    t, d = x.shape
    n = w.shape[1]
    n_tiles = n // d
    f_tile = 5

    def w_tile(i, j):
        return 0, jnp.where(j == 0, f_tile, jnp.where(j <= f_tile, j - 1, j))

    return pl.pallas_call(
        functools.partial(_even_proj_kernel, n_tiles=n_tiles),
        grid=(t // tm, n_tiles),
        in_specs=[pl.BlockSpec((tm, d), lambda i, j: (i, 0)),
                  pl.BlockSpec((1, d), lambda i, j: (0, 0)),
                  pl.BlockSpec((d, d), w_tile),
                  pl.BlockSpec((1, d), lambda i, j: (0, 0)),
                  pl.BlockSpec((1, d), lambda i, j: (0, 0))],
        out_specs=[pl.BlockSpec((tm, d), lambda i, j: (i, j)),
                   pl.BlockSpec((tm, d), lambda i, j: (i, 0)),
                   pl.BlockSpec((tm, d), lambda i, j: (i, 0))],
        out_shape=[jax.ShapeDtypeStruct((t, n), BF16),
                   jax.ShapeDtypeStruct((t, d), F32),
                   jax.ShapeDtypeStruct((t, d), BF16)],
        scratch_shapes=[pltpu.VMEM((tm, d), BF16), pltpu.VMEM((tm, d), F32)],
        compiler_params=_params("parallel", "arbitrary"),
        name="even_in_proj",
    )(x, nw.reshape(1, d), w, log_lb.reshape(1, d), log_1mlb.reshape(1, d))


def _qkv_conv_kernel(x_ref, nw_ref, w_ref, cw_ref, o_ref, xn_ref, halo_ref, *, tiles_per_seq):
    i = pl.program_id(0)
    j = pl.program_id(1)
    tm, tn = o_ref.shape
    halo = halo_ref.shape[1]

    pl.when(j == 0)(functools.partial(_norm_rows, x_ref, nw_ref, xn_ref))

    @pl.when(i == 0)
    def _():
        halo_ref[j] = jnp.zeros((halo, tn), F32)

    acc = _dot(xn_ref[...], w_ref[...])
    prev = jnp.where(i % tiles_per_seq == 0, 0.0, halo_ref[j])
    halo_ref[j] = acc[tm - halo:, :]
    ext = jnp.concatenate([prev, acc], axis=0)
    cw = cw_ref[...]
    conv = ext * cw[0:1, :]
    for k in range(1, CONV_K):
        conv = ext * cw[k:k + 1, :] + pltpu.roll(conv, 1, 0)
    y = _silu(conv[halo:, :])

    @pl.when(j < QK_TILES)
    def _():
        scale = jnp.where(j == 0, HEAD ** -0.5, 1.0)
        for h0 in range(0, tn, HEAD):
            yh = y[:, h0:h0 + HEAD]
            inv = lax.rsqrt(jnp.sum(yh * yh, axis=-1, keepdims=True) + RMS_EPS) * scale
            o_ref[:, h0:h0 + HEAD] = _bf(yh * inv)

    @pl.when(j >= QK_TILES)
    def _():
        o_ref[...] = _bf(y)


def _qkv_conv_proj(x, nw, w, conv_w, tm, seq):
    t, d = x.shape
    n = conv_w.shape[1]
    tn = d
    return pl.pallas_call(
        functools.partial(_qkv_conv_kernel, tiles_per_seq=seq // tm),
        grid=(t // tm, n // tn),
        in_specs=[pl.BlockSpec((tm, d), lambda i, j: (i, 0)),
                  pl.BlockSpec((1, d), lambda i, j: (0, 0)),
                  pl.BlockSpec((d, tn), lambda i, j: (0, j)),
                  pl.BlockSpec((CONV_K, tn), lambda i, j: (0, j))],
        out_specs=pl.BlockSpec((tm, tn), lambda i, j: (i, j)),
        out_shape=jax.ShapeDtypeStruct((t, n), BF16),
        scratch_shapes=[pltpu.VMEM((tm, d), BF16),
                        pltpu.VMEM((n // tn, SUBLANES, tn), F32)],
        compiler_params=_params("arbitrary", "arbitrary"),
        name="qkv_conv_proj",
    )(x, nw.reshape(1, d), w, conv_w)


def _out_proj_kernel(*refs, n_in, final):
    h_ref = refs[0]
    a_refs = refs[1:1 + n_in]
    w_refs = refs[1 + n_in:1 + 2 * n_in]
    o_ref = refs[-1]
    acc = h_ref[...]
    for a_ref, w_ref in zip(a_refs, w_refs):
        acc = acc + _dot(a_ref[...], w_ref[...])
    if final:
        fw_ref = refs[1 + 2 * n_in]
        ms = jnp.mean(acc * acc, axis=-1, keepdims=True)
        acc = acc * lax.rsqrt(ms + RMS_EPS) * fw_ref[...]
    o_ref[...] = acc


def _out_proj(h, acts, w, final_w, tm):
    t, d = h.shape
    n_in = len(acts)
    k = acts[0].shape[1]
    final = final_w is not None
    in_specs = [pl.BlockSpec((tm, d), lambda i: (i, 0))]
    in_specs += [pl.BlockSpec((tm, k), lambda i: (i, 0)) for _ in acts]
    in_specs += [pl.BlockSpec((k, d), lambda i, rb=rb: (rb, 0)) for rb in range(n_in)]
    args = [h, *acts] + [w] * n_in
    if final:
        in_specs.append(pl.BlockSpec((1, d), lambda i: (0, 0)))
        args.append(final_w.reshape(1, d))
    return pl.pallas_call(
        functools.partial(_out_proj_kernel, n_in=n_in, final=final),
        grid=(t // tm,),
        in_specs=in_specs,
        out_specs=pl.BlockSpec((tm, d), lambda i: (i, 0)),
        out_shape=jax.ShapeDtypeStruct((t, d), F32),
        compiler_params=_params("parallel"),
        name="out_proj",
    )(*args)


def _attn_kernel(q_ref, k_ref, v_ref, gate_ref, o_ref, qp_ref, kp_ref, vp_ref, acc_ref, m_ref, l_ref,
                 bias_ref, *, seq):
    per_res = seq // A_MOD
    n_blocks = seq // A_BLOCK
    row = lax.broadcasted_iota(jnp.int32, (A_BLOCK, 2 * A_BLOCK), 0)
    col = lax.broadcasted_iota(jnp.int32, (A_BLOCK, 2 * A_BLOCK), 1)
    lane = lax.broadcasted_iota(jnp.int32, (A_BLOCK, LANES), 1)
    head0 = lane < A_HEAD_DIM

    def widen(t, carry):
        rows = pl.ds(pl.multiple_of(t * A_BLOCK, A_BLOCK), A_BLOCK)
        acc_ref[rows, :] = q_ref[rows, :].astype(F32) * A_SCALE
        m_ref[rows, :] = k_ref[rows, :].astype(F32)
        l_ref[rows, :] = v_ref[rows, :].astype(F32)
        return carry

    lax.fori_loop(0, n_blocks, widen, 0)

    def to_residue_major(res, carry):
        src = pl.ds(res, per_res, stride=A_MOD)
        dst = pl.ds(pl.multiple_of(res * per_res, per_res), per_res)
        qp_ref[dst, :] = acc_ref[src, :]
        kp_ref[dst, :] = m_ref[src, :]
        vp_ref[dst, :] = l_ref[src, :]
        return carry

    lax.fori_loop(0, A_MOD, to_residue_major, 0)

    for dilation in A_DILATIONS:
        first = dilation == A_DILATIONS[0]
        per_class = n_blocks // dilation
        runs = A_MOD // dilation
        rq = A_BLOCK // runs
        dist = runs * (row % rq - col % (2 * rq)) + (row // rq - col // (2 * rq)) + A_BLOCK
        band = (dist >= 0) & (dist <= A_BLOCK)
        in_cur = col % (2 * rq) >= rq
        slot = 2 * A_DILATIONS.index(dilation)
        bias_ref[slot] = jnp.where(band & in_cur, 0.0, -jnp.inf)
        bias_ref[slot + 1] = jnp.where(band, 0.0, -jnp.inf)

        def load(n, dilation=dilation, first=first, per_class=per_class, runs=runs, rq=rq):
            res = n // per_class
            i = n % per_class
            cur, prev = [], []
            for c in range(runs):
                base = pl.multiple_of((c * dilation + res) * per_res + rq * i, SUBLANES)
                cur.append(pl.ds(base, rq))
                prev.append(pl.ds(pl.multiple_of(jnp.where(i > 0, base - rq, base), SUBLANES), rq))

            def window(ref):
                return _bf(jnp.concatenate([ref[rows, :] for pc in zip(prev, cur) for rows in pc], axis=0))

            def block(ref):
                return jnp.concatenate([ref[rows, :] for rows in cur], axis=0)

            return cur, i, block(qp_ref), window(kp_ref), window(vp_ref)

        def scores(i, q, kw, vw, slot=slot):
            bias = bias_ref[slot + jnp.where(i > 0, 1, 0)]
            parts = []
            for h in range(2):
                qh = _bf(jnp.where(head0 if h == 0 else ~head0, q, 0.0))
                s = _dot_nt(qh, kw) + bias
                m_blk = jnp.max(s, axis=-1, keepdims=True)
                p = jnp.exp2(s - m_blk)
                parts.append((_dot(_bf(p), vw), m_blk, jnp.sum(p, axis=-1, keepdims=True)))
            return tuple(jnp.where(head0, a, b) for a, b in zip(*parts))

        n_group = min(A_GROUP_FIRST if first else A_GROUP, n_blocks)

        def group(t, carry, load=load, scores=scores, first=first, rq=rq, n_group=n_group):
            loaded = [load(t * n_group + u) for u in range(n_group)]
            results = [scores(*item[1:]) for item in loaded]
            for (cur, *_), (acc, m_blk, l_blk) in zip(loaded, results):
                if not first:
                    acc_old, m_old, l_old = (
                        jnp.concatenate([ref[rows, :] for rows in cur], axis=0)
                        for ref in (acc_ref, m_ref, l_ref))
                    m_new = jnp.maximum(m_old, m_blk)
                    alpha = jnp.exp2(m_old - m_new)
                    beta = jnp.exp2(m_blk - m_new)
                    acc = acc_old * alpha + acc * beta
                    l_blk = l_old * alpha + l_blk * beta
                    m_blk = m_new
                for ref, val in zip((acc_ref, m_ref, l_ref), (acc, m_blk, l_blk)):
                    for c, rows in enumerate(cur):
                        ref[rows, :] = val[c * rq:(c + 1) * rq, :]
            return carry

        lax.fori_loop(0, n_blocks // n_group, group, 0)

    def to_position_order(res, carry):
        src = pl.ds(pl.multiple_of(res * per_res, per_res), per_res)
        m_ref[pl.ds(res, per_res, stride=A_MOD), :] = acc_ref[src, :] / l_ref[src, :]
        return carry

    lax.fori_loop(0, A_MOD, to_position_order, 0)

    def finish(t, carry):
        rows = pl.ds(pl.multiple_of(t * A_BLOCK, A_BLOCK), A_BLOCK)
        o_ref[rows, :] = _bf(m_ref[rows, :] * _silu(gate_ref[rows, :].astype(F32)))
        return carry

    lax.fori_loop(0, n_blocks, finish, 0)


def _dilated_attention(proj):
    bsz, s, _ = proj.shape
    hp_blocks = D_MODEL // LANES

    def col_spec(off):
        return pl.BlockSpec((None, s, LANES), lambda b, hp: (b, 0, (off + 1) * hp_blocks + hp))

    return pl.pallas_call(
        functools.partial(_attn_kernel, seq=s),
        grid=(bsz, hp_blocks),
        in_specs=[col_spec(0), col_spec(1), col_spec(2), col_spec(3)],
        out_specs=pl.BlockSpec((None, s, LANES), lambda b, hp: (b, 0, hp)),
        out_shape=jax.ShapeDtypeStruct((bsz, s, D_MODEL), BF16),
        scratch_shapes=[pltpu.VMEM((s, LANES), F32)] * 6
        + [pltpu.VMEM((2 * len(A_DILATIONS), A_BLOCK, 2 * A_BLOCK), F32)],
        compiler_params=_params("parallel", "parallel"),
        name="dilated_attn",
    )(proj, proj, proj, proj)


def _hgrn_kernel(q_ref, logf_ref, k_ref, i_ref, gate_ref, nw_ref, o_ref, st_ref, *, tb):
    @pl.when(pl.program_id(2) == 0)
    def _():
        st_ref[...] = jnp.zeros_like(st_ref)

    n_ch = tb // CHUNK
    r64 = lax.broadcasted_iota(jnp.int32, (CHUNK, CHUNK), 0)
    c64 = lax.broadcasted_iota(jnp.int32, (CHUNK, CHUNK), 1)
    tri = _bf(jnp.where(r64 >= c64, 1.0, 0.0))
    row = lax.broadcasted_iota(jnp.int32, (CHUNK, HEAD), 0)
    sub = lax.broadcasted_iota(jnp.int32, (SUBLANES, HEAD), 0)
    units = [(hd, slice(c * CHUNK, (c + 1) * CHUNK), slice(hd * HEAD, (hd + 1) * HEAD))
             for c in range(n_ch) for hd in range(HGRN_KH)]
    n_units = len(units)
    qs = [q_ref[rows, cols].astype(F32) for _, rows, cols in units]
    ks = [k_ref[rows, cols].astype(F32) for _, rows, cols in units]
    vbs = [i_ref[rows, cols] for _, rows, cols in units]

    gs = []
    for _, rows, cols in units:
        lf = logf_ref[rows, cols]
        hi = _bf(lf)
        r1 = lf - hi.astype(F32)
        mid = _bf(r1)
        lo = _bf(r1 - mid.astype(F32))
        cum = _dot(tri, jnp.concatenate([hi, mid, lo], axis=1))
        gs.append(cum[:, :HEAD] + cum[:, HEAD:2 * HEAD] + cum[:, 2 * HEAD:])

    attn = [None] * n_units
    s = CHUNK // 2
    while s >= 1:
        blk = 2 * s
        upper = (row % blk) >= s
        mask = ((r64 // blk) == (c64 // blk)) & ((r64 % blk) >= s) & ((c64 % blk) < s)
        for c in range(n_units):
            g = gs[c]
            if blk >= SUBLANES:
                g_ref = jnp.concatenate(
                    [jnp.broadcast_to(g[b0 + s - 1:b0 + s, :], (blk, HEAD)) for b0 in range(0, CHUNK, blk)],
                    axis=0)
            elif s == 2:
                g_ref = jnp.concatenate(
                    [jnp.where(sub < 4, jnp.broadcast_to(g[v0 + 1:v0 + 2, :], (SUBLANES, HEAD)),
                               jnp.broadcast_to(g[v0 + 5:v0 + 6, :], (SUBLANES, HEAD)))
                     for v0 in range(0, CHUNK, SUBLANES)], axis=0)
            else:
                g_ref = jnp.where(upper, pltpu.roll(g, 1, 0), g)
            z = _bf(jnp.where(upper, qs[c], ks[c]) * jnp.exp(-jnp.abs(g - g_ref)))
            m = _dot_nt(z, z)
            attn[c] = jnp.where(mask, m, 0.0) if attn[c] is None else jnp.where(mask, m, attn[c])
        s //= 2
    eye = r64 == c64
    attn = [jnp.where(eye, jnp.sum(qs[c] * ks[c], axis=-1, keepdims=True), attn[c]) for c in range(n_units)]

    o_intra = [_dot(_bf(attn[c]), vbs[c]) for c in range(n_units)]
    q_dec = [_bf(qs[c] * jnp.exp(gs[c])) for c in range(n_units)]
    g_last = [gs[c][CHUNK - 1:CHUNK, :] for c in range(n_units)]
    kv = [_dot_tn(vbs[c], _bf(ks[c] * jnp.exp(g_last[c] - gs[c]))) for c in range(n_units)]

    states = [st_ref[hd] for hd in range(HGRN_KH)]
    outs = []
    for c, (hd, _, _) in enumerate(units):
        outs.append(_dot_nt(q_dec[c], _bf(states[hd])) + o_intra[c])
        states[hd] = states[hd] * jnp.exp(g_last[c]) + kv[c]
    for hd in range(HGRN_KH):
        st_ref[hd] = states[hd]

    for c, (_, rows, cols) in enumerate(units):
        o = outs[c]
        ms = jnp.mean(o * o, axis=-1, keepdims=True)
        o = o * lax.rsqrt(ms + RMS_EPS) * nw_ref[...]
        o_ref[rows, cols] = _bf(o * _silu(gate_ref[rows, cols].astype(F32)))


def _hgrn(proj, log_f, k, norm_w, tb):
    bsz, s, _ = proj.shape
    width = HGRN_KH * HEAD
    groups = D_MODEL // width
    base = 5 * groups

    def col_spec(off):
        return pl.BlockSpec((None, tb, width), lambda b, h, t: (b, t, base + off * groups + h))

    head_spec = pl.BlockSpec((None, tb, width), lambda b, h, t: (b, t, h))
    return pl.pallas_call(
        functools.partial(_hgrn_kernel, tb=tb),
        grid=(bsz, groups, s // tb),
        in_specs=[col_spec(0), head_spec, head_spec, col_spec(1), col_spec(2),
                  pl.BlockSpec((1, HEAD), lambda b, h, t: (0, 0))],
        out_specs=head_spec,
        out_shape=jax.ShapeDtypeStruct((bsz, s, D_MODEL), BF16),
        scratch_shapes=[pltpu.VMEM((HGRN_KH, HEAD, HEAD), F32)],
        compiler_params=_params("parallel", "parallel", "arbitrary"),
        name="hgrn2",
    )(proj, log_f, k, proj, proj, norm_w.reshape(1, HEAD))


def _unit_lower_inverses(lows):
    r = lax.broadcasted_iota(jnp.int32, (CHUNK, CHUNK), 0)
    c = lax.broadcasted_iota(jnp.int32, (CHUNK, CHUNK), 1)
    eye = jnp.where(r == c, 1.0, 0.0).astype(F32)
    s = 1
    invs = None
    while s < CHUNK:
        sel = ((r // (2 * s)) == (c // (2 * s))) & ((r % (2 * s)) >= s) & ((c % (2 * s)) < s)
        low_s = [jnp.where(sel, low, 0.0) for low in lows]
        if s == 1:
            invs = [eye - x for x in low_s]
        else:
            inv_b = [_bf(x) for x in invs]
            tmp = [_bf(_dot(a, _bf(x))) for a, x in zip(inv_b, low_s)]
            invs = [x - _dot(t, a) for x, t, a in zip(invs, tmp, inv_b)]
        s *= 2
    return invs


def _gdn_kernel(q_ref, k_ref, v_ref, z_ref, gates_ref, avec_ref, dtb_ref, nw_ref, o_ref, st_ref, *, tb):
    tt = pl.program_id(2)
    group = pl.program_id(1)
    n_ch = tb // CHUNK

    @pl.when(tt == 0)
    def _():
        st_ref[...] = jnp.zeros_like(st_ref)

    q_all = [q_ref[:, kk * HEAD:(kk + 1) * HEAD].astype(F32) for kk in range(GDN_KH)]
    k_all = [k_ref[:, kk * HEAD:(kk + 1) * HEAD].astype(F32) for kk in range(GDN_KH)]

    gl = gates_ref[...]
    beta_all = jax.nn.sigmoid(gl)
    g_all = avec_ref[...] * _softplus(gl + dtb_ref[...])
    lane = lax.broadcasted_iota(jnp.int32, (tb, LANES), 1)

    r64 = lax.broadcasted_iota(jnp.int32, (CHUNK, CHUNK), 0)
    c64 = lax.broadcasted_iota(jnp.int32, (CHUNK, CHUNK), 1)
    tri = _bf(jnp.where(r64 >= c64, 1.0, 0.0))
    causal = r64 >= c64
    strict = r64 > c64
    strict_f = jnp.where(strict, 1.0, 0.0).astype(F32)

    chunk_rows = [slice(c * CHUNK, (c + 1) * CHUNK) for c in range(n_ch)]
    n_vh = 2 * GDN_KH
    units = [(vh, c) for c in range(n_ch) for vh in range(n_vh)]

    beta_cols, g_cols = [], []
    for vh in range(n_vh):
        head = n_vh * group + vh
        beta_cols.append(jnp.sum(jnp.where(lane == head, beta_all, 0.0), axis=-1, keepdims=True))
        g_cols.append(jnp.sum(jnp.where(lane == N_V_HEADS + head, g_all, 0.0), axis=-1, keepdims=True))

    kq = [[_dot_nt(_bf(jnp.concatenate([k_all[kk][rows, :], q_all[kk][rows, :]], axis=0)),
                   _bf(k_all[kk][rows, :])) for rows in chunk_rows]
          for kk in range(GDN_KH)]

    decay, g_i, g_last = [], [], []
    for vh, c in units:
        gb = jnp.broadcast_to(g_cols[vh][chunk_rows[c], :], (CHUNK, CHUNK))
        wmat = jnp.concatenate([gb * strict_f, gb], axis=1)
        hi = _bf(wmat)
        lo = _bf(wmat - hi.astype(F32))
        cum = _dot(tri, hi) + _dot(tri, lo)
        decay.append(jnp.where(causal, jnp.exp(jnp.minimum(cum[:, :CHUNK], 0.0)), 0.0))
        g_i.append(cum[:, CHUNK:CHUNK + 1])
        g_last.append(cum[CHUNK - 1:CHUNK, CHUNK:CHUNK + 1])

    lows = [jnp.where(strict, kq[vh // 2][c][:CHUNK, :] * beta_cols[vh][chunk_rows[c], :] * decay[n], 0.0)
            for n, (vh, c) in enumerate(units)]
    invs = _unit_lower_inverses(lows)

    sols = []
    for n, (vh, c) in enumerate(units):
        rows = chunk_rows[c]
        beta = beta_cols[vh][rows, :]
        v = v_ref[rows, vh * HEAD:(vh + 1) * HEAD].astype(F32)
        rhs = jnp.concatenate([v * beta, k_all[vh // 2][rows, :] * (beta * jnp.exp(g_i[n]))], axis=1)
        sols.append(_bf(_dot(_bf(invs[n]), _bf(rhs))))

    o_loc, q_eff, c_mat, p_mat = [], [], [], []
    for n, (vh, c) in enumerate(units):
        rows = chunk_rows[c]
        attn = _bf(kq[vh // 2][c][CHUNK:, :] * decay[n])
        auw = _dot(attn, sols[n])
        o_loc.append(auw[:, :HEAD])
        q_eff.append(_bf(q_all[vh // 2][rows, :] * jnp.exp(g_i[n]) - auw[:, HEAD:]))
        k_tail = _bf(k_all[vh // 2][rows, :] * jnp.exp(g_last[n] - g_i[n]))
        ktuw = _dot_tn(k_tail, sols[n])
        c_mat.append(ktuw[:, :HEAD])
        p_mat.append(_bf(ktuw[:, HEAD:]))

    states = [st_ref[vh] for vh in range(n_vh)]
    outs = [None] * len(units)
    for n, (vh, c) in enumerate(units):
        sb = _bf(states[vh])
        outs[n] = _dot(q_eff[n], sb) + o_loc[n]
        states[vh] = states[vh] * jnp.exp(g_last[n]) + c_mat[n] - _dot(p_mat[n], sb)
    for vh in range(n_vh):
        st_ref[vh] = states[vh]

    for n, (vh, c) in enumerate(units):
        rows = chunk_rows[c]
        o = outs[n]
        ms = jnp.mean(o * o, axis=-1, keepdims=True)
        o = o * lax.rsqrt(ms + RMS_EPS) * nw_ref[...]
        z = z_ref[rows, vh * HEAD:(vh + 1) * HEAD].astype(F32)
        o_ref[rows, vh * HEAD:(vh + 1) * HEAD] = _bf(o * _silu(z))


def _gdn(qkv, z, gate_logits, a_vec, dtb_vec, norm_w, tb):
    bsz, s, _ = qkv.shape
    width = GDN_KH * HEAD
    groups = D_MODEL // width

    in_specs = [
        pl.BlockSpec((None, tb, width), lambda b, h, t: (b, t, h)),
        pl.BlockSpec((None, tb, width), lambda b, h, t: (b, t, groups + h)),
        pl.BlockSpec((None, tb, 2 * width), lambda b, h, t: (b, t, groups + h)),
        pl.BlockSpec((None, tb, 2 * width), lambda b, h, t: (b, t, h)),
        pl.BlockSpec((None, tb, LANES), lambda b, h, t: (b, t, 0)),
        pl.BlockSpec((1, LANES), lambda b, h, t: (0, 0)),
        pl.BlockSpec((1, LANES), lambda b, h, t: (0, 0)),
        pl.BlockSpec((1, HEAD), lambda b, h, t: (0, 0)),
    ]
    return pl.pallas_call(
        functools.partial(_gdn_kernel, tb=tb),
        grid=(bsz, groups, s // tb),
        in_specs=in_specs,
        out_specs=pl.BlockSpec((None, tb, 2 * width), lambda b, h, t: (b, t, h)),
        out_shape=jax.ShapeDtypeStruct((bsz, s, 2 * D_MODEL), BF16),
        scratch_shapes=[pltpu.VMEM((2 * GDN_KH, HEAD, HEAD), F32)],
        compiler_params=_params("parallel", "parallel", "arbitrary"),
        name="gated_deltanet",
    )(qkv, qkv, qkv, z, gate_logits, a_vec, dtb_vec, norm_w.reshape(1, HEAD))


def kernel(x, norm_w, final_norm_w, even_w_in, even_w_out, hgrn_lb_logits, hgrn_norm_w,
           odd_w_in, odd_conv_w, odd_dt_bias, odd_a_log, odd_norm_w, odd_w_out):
    bsz, s, d = x.shape
    depth = norm_w.shape[0]
    t = bsz * s
    tm = 1024
    tm_in = 1024
    tb = 512

    lb_all = jnp.cumsum(jax.nn.softmax(hgrn_lb_logits.astype(F32), axis=0), axis=0)
    lb_all = jnp.maximum(lb_all - lb_all[0:1], 0.0)
    log_lb = jnp.log(lb_all)
    log_1mlb = jnp.log1p(-lb_all)

    h = x.reshape(t, d)
    for layer in range(depth):
        j = layer // 2
        final_w = final_norm_w if layer == depth - 1 else None
        if layer % 2 == 0:
            proj, log_f, k = _even_proj(h, norm_w[layer], _bf(even_w_in[j]), log_lb[j], log_1mlb[j],
                                        tm_in)
            proj = proj.reshape(bsz, s, EVEN_IN)
            a_mix = _dilated_attention(proj)
            b_mix = _hgrn(proj, log_f.reshape(bsz, s, d), k.reshape(bsz, s, d), hgrn_norm_w[j], 2 * tb)
            h = _out_proj(h, [a_mix.reshape(t, d), b_mix.reshape(t, d)], _bf(even_w_out[j]),
                          final_w, tm)
        else:
            w_in = _bf(odd_w_in[j])
            pad = jnp.zeros((d, ODD_REST_PAD - (w_in.shape[1] - ODD_CONV)), BF16)
            w_rest = jnp.concatenate([w_in[:, ODD_CONV:], pad], axis=1)
            qkv = _qkv_conv_proj(h, norm_w[layer], w_in, odd_conv_w[j], tm_in, s)
            z, gate_logits = _norm_matmul(h, norm_w[layer], w_rest, tm_in, ODD_REST_PAD,
                                          side_start=ODD_REST_PAD - LANES, side_width=LANES)
            zeros16 = jnp.zeros((N_V_HEADS,), F32)
            tail = jnp.zeros((LANES - 2 * N_V_HEADS,), F32)
            a_vec = jnp.concatenate([zeros16, -jnp.exp(odd_a_log[j].astype(F32)), tail]).reshape(1, LANES)
            dtb_vec = jnp.concatenate([zeros16, odd_dt_bias[j].astype(F32), tail]).reshape(1, LANES)
            o = _gdn(qkv.reshape(bsz, s, ODD_CONV), z.reshape(bsz, s, ODD_REST_PAD),
                     gate_logits.reshape(bsz, s, LANES), a_vec, dtb_vec, odd_norm_w[j], tb // 4)
            h = _out_proj(h, [o.reshape(t, 2 * d)], _bf(odd_w_out[j]), final_w, tm)
    return h.reshape(bsz, s, d)
```
